```python
import jax, jax.numpy as jnp
from jax import lax
import numpy as np

D_MODEL = 1024
BATCH = 8
SEQ = 2048
DEPTH = 1
DEC_BATCH = 8
DEC_SEQ = 8192
PAST_LEN = 128

MIX_WIDTH = D_MODEL
ATTN_WIDTH = MIX_WIDTH // 2
CONV_WIDTH = MIX_WIDTH - ATTN_WIDTH
HEAD_DIM = 64
N_HEADS = ATTN_WIDTH // HEAD_DIM
DILATED_BRANCHES = ((128, 1), (512, 4), (2048, 16))
CONV_KERNEL = 31
D_FF = 2816
FFN_RESIDUAL_WEIGHT = 0.5
IN_PROJ_WIDTH = 3 * ATTN_WIDTH + 2 * CONV_WIDTH
RMS_EPS = 1e-6
LN_EPS = 1e-5
NEG_INF = -1e30

kernel_name = "hymba_longnet_conformer_encoder"


def rms_norm(x, g):
    x32 = x.astype(jnp.float32)
    y = x32 * lax.rsqrt(jnp.mean(x32 * x32, axis=-1, keepdims=True) + RMS_EPS)
    return (y * g.astype(jnp.float32)).astype(x.dtype)


def layer_norm(x, g, b):
    x32 = x.astype(jnp.float32)
    mu = jnp.mean(x32, axis=-1, keepdims=True)
    var = jnp.mean(jnp.square(x32 - mu), axis=-1, keepdims=True)
    y = (x32 - mu) * lax.rsqrt(var + LN_EPS)
    return (y * g.astype(jnp.float32) + b.astype(jnp.float32)).astype(x.dtype)


def alibi_slopes(n_heads):
    return 2.0 ** (-8.0 * jnp.arange(1, n_heads + 1, dtype=jnp.float32) / n_heads)


def swiglu(h, w_gu, w_down):
    gate, up = jnp.split(h @ w_gu, 2, axis=-1)
    return (jax.nn.silu(gate) * up) @ w_down


def dilated_branch(q, k, v, slopes, dilation, half):
    B, S, H, Dh = q.shape
    L = S // dilation
    N = B * dilation
    nb = -(-L // half)
    Lp = nb * half

    def strided(a):
        return a.reshape(B, L, dilation, H, Dh).transpose(0, 2, 1, 3, 4).reshape(N, L, H, Dh)

    qs = jnp.pad(strided(q), ((0, 0), (0, Lp - L), (0, 0), (0, 0))).reshape(N, nb, half, H, Dh)

    def key_blocks(a):
        ap = jnp.pad(strided(a), ((0, 0), (half, Lp - L + half), (0, 0), (0, 0))).reshape(N, nb + 2, half, H, Dh)
        return jnp.concatenate([ap[:, :-2], ap[:, 1:-1], ap[:, 2:]], axis=2)

    kb, vb = key_blocks(k), key_blocks(v)
    s = jnp.einsum('nbqhd,nbkhd->nbhqk', qs, kb).astype(jnp.float32)
    rel = jnp.arange(3 * half)[None, :] - half - jnp.arange(half)[:, None]
    kpos = jnp.arange(nb)[:, None] * half - half + jnp.arange(3 * half)[None, :]
    valid = (jnp.abs(rel) <= half)[None, :, :] & ((kpos >= 0) & (kpos < L))[:, None, :]
    bias = -(slopes * dilation)[:, None, None] * jnp.abs(rel).astype(jnp.float32)[None]
    s = jnp.where(valid[None, :, None], s + bias[None, None], NEG_INF)
    m = jnp.max(s, axis=-1, keepdims=True)
    p = jnp.exp(s - m)
    den = jnp.sum(p, axis=-1, keepdims=True)
    o = jnp.einsum('nbhqk,nbkhd->nbqhd', (p / den).astype(v.dtype), vb)
    lse = jnp.transpose((m + jnp.log(den))[..., 0], (0, 1, 3, 2))
    o = o.reshape(N, Lp, H, Dh)[:, :L].reshape(B, dilation, L, H, Dh).transpose(0, 2, 1, 3, 4).reshape(B, S, H, Dh)
    lse = lse.reshape(N, Lp, H)[:, :L].reshape(B, dilation, L, H).transpose(0, 2, 1, 3).reshape(B, S, H)
    return o, lse


def dilated_attention(q, k, v):
    slopes = alibi_slopes(N_HEADS)
    outs, lses = [], []
    for window, dilation in DILATED_BRANCHES:
        o, lse = dilated_branch(q, k, v, slopes, dilation, window // (2 * dilation))
        outs.append(o.astype(jnp.float32))
        lses.append(lse)
    w = jax.nn.softmax(jnp.stack(lses), axis=0)
    o = jnp.sum(w[..., None] * jnp.stack(outs), axis=0)
    return o.astype(q.dtype)


def mixer(h, w_in, conv_w, conv_b, ln_g, ln_b, w_out):
    B, S, _ = h.shape
    proj = h @ w_in
    q, k, v, ga, gb = jnp.split(proj, [ATTN_WIDTH, 2 * ATTN_WIDTH, 3 * ATTN_WIDTH, 3 * ATTN_WIDTH + CONV_WIDTH], axis=-1)
    q = q.reshape(B, S, N_HEADS, HEAD_DIM) * (HEAD_DIM ** -0.5)
    k = k.reshape(B, S, N_HEADS, HEAD_DIM)
    v = v.reshape(B, S, N_HEADS, HEAD_DIM)
    attn = dilated_attention(q, k, v).reshape(B, S, ATTN_WIDTH)
    glu = ga * jax.nn.sigmoid(gb)
    pad = CONV_KERNEL // 2
    c = lax.conv_general_dilated(glu, conv_w.astype(glu.dtype), window_strides=(1,), padding=[(pad, pad)],
                                 dimension_numbers=('NWC', 'WIO', 'NWC'), feature_group_count=CONV_WIDTH) + conv_b
    c = jax.nn.silu(layer_norm(c, ln_g, ln_b))
    return jnp.concatenate([attn, c], axis=-1) @ w_out


def trunk(x, ffn1_pre_g, ffn1_w_gu, ffn1_w_down, ffn1_post_g, mix_pre_g, w_in, conv_w, conv_b, conv_ln_g,
          conv_ln_b, w_out, mix_post_g, ffn2_pre_g, ffn2_w_gu, ffn2_w_down, ffn2_post_g, final_g):
    for l in range(DEPTH):
        x = x + FFN_RESIDUAL_WEIGHT * rms_norm(swiglu(rms_norm(x, ffn1_pre_g[l]), ffn1_w_gu[l], ffn1_w_down[l]), ffn1_post_g[l])
        x = x + rms_norm(mixer(rms_norm(x, mix_pre_g[l]), w_in[l], conv_w[l], conv_b[l], conv_ln_g[l], conv_ln_b[l], w_out[l]), mix_post_g[l])
        x = x + FFN_RESIDUAL_WEIGHT * rms_norm(swiglu(rms_norm(x, ffn2_pre_g[l]), ffn2_w_gu[l], ffn2_w_down[l]), ffn2_post_g[l])
        x = rms_norm(x, final_g[l])
    return x


def setup_inputs(seed: int = 0) -> dict:
    key = jax.random.key(seed)
    ks = jax.random.split(key, 24)
    f32 = jnp.float32

    def nrm(k, shape, scale):
        return jax.random.normal(k, shape, f32) * scale

    def gain(k, n):
        return 1.0 + 0.1 * jax.random.normal(k, (DEPTH, n), f32)

    return {
        "x_prompt": jax.random.normal(ks[0], (BATCH, SEQ, D_MODEL), f32),
        "x_sample": jax.random.normal(ks[1], (DEC_BATCH, DEC_SEQ, D_MODEL), f32),
        "ffn1_pre_g": gain(ks[2], D_MODEL),
        "ffn1_w_gu": nrm(ks[3], (DEPTH, D_MODEL, 2 * D_FF), D_MODEL ** -0.5),
        "ffn1_w_down": nrm(ks[4], (DEPTH, D_FF, D_MODEL), D_FF ** -0.5),
        "ffn1_post_g": gain(ks[5], D_MODEL),
        "mix_pre_g": gain(ks[6], D_MODEL),
        "w_in": nrm(ks[7], (DEPTH, D_MODEL, IN_PROJ_WIDTH), D_MODEL ** -0.5),
        "conv_w": nrm(ks[8], (DEPTH, CONV_KERNEL, 1, CONV_WIDTH), CONV_KERNEL ** -0.5),
        "conv_b": nrm(ks[9], (DEPTH, CONV_WIDTH), 0.02),
        "conv_ln_g": gain(ks[10], CONV_WIDTH),
        "conv_ln_b": nrm(ks[11], (DEPTH, CONV_WIDTH), 0.02),
        "w_out": nrm(ks[12], (DEPTH, MIX_WIDTH, D_MODEL), MIX_WIDTH ** -0.5),
        "mix_post_g": gain(ks[13], D_MODEL),
        "ffn2_pre_g": gain(ks[14], D_MODEL),
        "ffn2_w_gu": nrm(ks[15], (DEPTH, D_MODEL, 2 * D_FF), D_MODEL ** -0.5),
        "ffn2_w_down": nrm(ks[16], (DEPTH, D_FF, D_MODEL), D_FF ** -0.5),
        "ffn2_post_g": gain(ks[17], D_MODEL),
        "final_g": gain(ks[18], D_MODEL),
    }


def reference(x_prompt, x_sample, ffn1_pre_g, ffn1_w_gu, ffn1_w_down, ffn1_post_g, mix_pre_g, w_in, conv_w,
              conv_b, conv_ln_g, conv_ln_b, w_out, mix_post_g, ffn2_pre_g, ffn2_w_gu, ffn2_w_down, ffn2_post_g,
              final_g):
    y_prompt = trunk(x_prompt, ffn1_pre_g, ffn1_w_gu, ffn1_w_down, ffn1_post_g, mix_pre_g, w_in, conv_w, conv_b,
                     conv_ln_g, conv_ln_b, w_out, mix_post_g, ffn2_pre_g, ffn2_w_gu, ffn2_w_down, ffn2_post_g, final_g)
    y_sample = trunk(x_sample, ffn1_pre_g, ffn1_w_gu, ffn1_w_down, ffn1_post_g, mix_pre_g, w_in, conv_w, conv_b,
                     conv_ln_g, conv_ln_b, w_out, mix_post_g, ffn2_pre_g, ffn2_w_gu, ffn2_w_down, ffn2_post_g, final_g)
    return (y_prompt, y_sample)
```

```python
import functools

import jax
import jax.numpy as jnp
from jax import lax
from jax.experimental import pallas as pl
from jax.experimental.pallas import tpu as pltpu

D_MODEL = 1024
D_FF = 2816
ATTN_WIDTH = 512
CONV_WIDTH = 512
HEAD_DIM = 64
N_HEADS = ATTN_WIDTH // HEAD_DIM
HEAD_PAIRS = N_HEADS // 2
DILATED_BRANCHES = ((128, 1), (512, 4), (2048, 16))
HALF = 64
CONV_KERNEL = 31
CONV_PAD = CONV_KERNEL // 2
FFN_RESIDUAL_WEIGHT = 0.5
RMS_EPS = 1e-6
LN_EPS = 1e-5
NEG_INF = -1e30

LANES = 128
Q_SUB = 2 * HALF
K_WIN = Q_SUB + 2 * HALF
FF_CHUNK = 256
N_FF_CHUNKS = D_FF // FF_CHUNK
CONV_HALO = 16
CONV_ROWS = 32
VMEM_LIMIT_BYTES = 56 * 1024 * 1024

assert all(w // (2 * d) == HALF for w, d in DILATED_BRANCHES)
assert D_FF % FF_CHUNK == 0

_BF16 = jnp.bfloat16
_F32 = jnp.float32


def _rms(x, g):
    ms = jnp.mean(x * x, axis=-1, keepdims=True)
    return x * lax.rsqrt(ms + RMS_EPS) * g


def _whole_vmem():
    return pl.BlockSpec(memory_space=pltpu.VMEM)


def _ffn_body(x_ref, pre_g, wg, wu, wd, post_g, h_scr, a_scr):
    h_scr[...] = _rms(x_ref[...], pre_g[...]).astype(_BF16)
    for c in range(N_FF_CHUNKS):
        h = h_scr[...]
        gate = jnp.dot(h, wg[c], preferred_element_type=_F32)
        up = jnp.dot(h, wu[c], preferred_element_type=_F32)
        a_scr[:, c * FF_CHUNK:(c + 1) * FF_CHUNK] = (gate * jax.nn.sigmoid(gate) * up).astype(_BF16)
    y = jnp.dot(a_scr[...], wd[...], preferred_element_type=_F32)
    return x_ref[...] + FFN_RESIDUAL_WEIGHT * _rms(y, post_g[...])


def _ffn_proj_kernel(x_ref, pre_g, wg, wu, wd, post_g, mix_g, wq, wk, wv, wa, wb,
                     x1_ref, q_ref, k_ref, v_ref, glu_ref, h_scr, a_scr):
    x1 = _ffn_body(x_ref, pre_g, wg, wu, wd, post_g, h_scr, a_scr)
    x1_ref[...] = x1
    h_scr[...] = _rms(x1, mix_g[...]).astype(_BF16)
    q_ref[...] = (jnp.dot(h_scr[...], wq[...], preferred_element_type=_F32) * (HEAD_DIM ** -0.5)).astype(_BF16)
    k_ref[...] = jnp.dot(h_scr[...], wk[...], preferred_element_type=_F32).astype(_BF16)
    v_ref[...] = jnp.dot(h_scr[...], wv[...], preferred_element_type=_F32).astype(_BF16)
    ga = jnp.dot(h_scr[...], wa[...], preferred_element_type=_F32)
    gb = jnp.dot(h_scr[...], wb[...], preferred_element_type=_F32)
    glu_ref[...] = ga * jax.nn.sigmoid(gb)


def _ffn_final_kernel(x_ref, pre_g, wg, wu, wd, post_g, final_g, y_ref, h_scr, a_scr):
    x1 = _ffn_body(x_ref, pre_g, wg, wu, wd, post_g, h_scr, a_scr)
    y_ref[...] = _rms(x1, final_g[...])


def _ffn_tile(n_tokens):
    return 512 if n_tokens % 512 == 0 else 256


def _ffn_call(body, x, weights, out_widths_dtypes):
    n_tokens = x.shape[0]
    tm = _ffn_tile(n_tokens)
    row_spec = lambda w: pl.BlockSpec((tm, w), lambda i: (i, 0))
    return pl.pallas_call(
        body,
        name=body.__name__.strip("_"),
        grid=(n_tokens // tm,),
        in_specs=[row_spec(D_MODEL)] + [_whole_vmem() for _ in weights],
        out_specs=[row_spec(w) for w, _ in out_widths_dtypes],
        out_shape=[jax.ShapeDtypeStruct((n_tokens, w), dt) for w, dt in out_widths_dtypes],
        scratch_shapes=[pltpu.VMEM((tm, D_MODEL), _BF16), pltpu.VMEM((tm, D_FF), _BF16)],
        compiler_params=pltpu.CompilerParams(dimension_semantics=("arbitrary",),
                                             vmem_limit_bytes=VMEM_LIMIT_BYTES),
    )(x, *weights)


def _attn_kernel(q_ref, kp_ref, k_ref, kn_ref, vp_ref, v_ref, vn_ref, o_ref, lse_ref, bias_scr,
                 *, dilation, rows):
    j = pl.program_id(2)
    n_sub = rows // Q_SUB

    @pl.when((pl.program_id(0) == 0) & (pl.program_id(1) == 0) & (j == 0))
    def _():
        row = lax.broadcasted_iota(jnp.int32, (2 * Q_SUB, K_WIN), 0)
        col = lax.broadcasted_iota(jnp.int32, (2 * Q_SUB, K_WIN), 1)
        rel = jnp.abs(col - HALF - (row & (Q_SUB - 1)))
        dist = rel.astype(_F32)
        for p in range(HEAD_PAIRS):
            slope_even = dilation * 2.0 ** (-8.0 * (2 * p + 1) / N_HEADS)
            slope_odd = dilation * 2.0 ** (-8.0 * (2 * p + 2) / N_HEADS)
            slope = jnp.where(row < Q_SUB, slope_even, slope_odd)
            bias_scr[p] = jnp.where(rel <= HALF, -slope * dist, NEG_INF)

    lane = lax.broadcasted_iota(jnp.int32, (Q_SUB, LANES), 1)
    even_lanes = lane < HEAD_DIM
    col = lax.broadcasted_iota(jnp.int32, (2 * Q_SUB, K_WIN), 1)
    first_valid_col = jnp.where(j == 0, HALF, 0)
    last_valid_col = jnp.where(j == pl.num_programs(2) - 1, K_WIN - HALF, K_WIN)
    ones = jnp.ones((K_WIN, LANES), _BF16)

    def window(prev_ref, main_ref, next_ref, i, lanes):
        parts = []
        lo = i * Q_SUB - HALF
        hi = lo + K_WIN
        if lo < 0:
            parts.append(prev_ref[0, :, lanes])
            lo = 0
        parts.append(main_ref[0, lo:min(hi, rows), lanes])
        if hi > rows:
            parts.append(next_ref[0, :, lanes])
        return parts[0] if len(parts) == 1 else jnp.concatenate(parts, axis=0)

    for i in range(n_sub):
        q_rows = slice(i * Q_SUB, (i + 1) * Q_SUB)
        for p in range(HEAD_PAIRS):
            lanes = slice(p * LANES, (p + 1) * LANES)
            q2 = q_ref[0, q_rows, lanes]
            zero = jnp.zeros_like(q2)
            q_stack = jnp.concatenate([jnp.where(even_lanes, q2, zero), jnp.where(even_lanes, zero, q2)], axis=0)
            kw = window(kp_ref, k_ref, kn_ref, i, lanes)
            s = lax.dot_general(q_stack, kw, (((1,), (1,)), ((), ())), preferred_element_type=_F32)
            s = s + bias_scr[p]
            if i == 0:
                s = jnp.where(col < first_valid_col, NEG_INF, s)
            if i == n_sub - 1:
                s = jnp.where(col >= last_valid_col, NEG_INF, s)
            m = jnp.max(s, axis=-1, keepdims=True)
            e = jnp.exp(s - m).astype(_BF16)
            vw = window(vp_ref, v_ref, vn_ref, i, lanes)
            r = jnp.dot(e, jnp.concatenate([vw, ones], axis=1), preferred_element_type=_F32)
            acc = jnp.where(even_lanes, r[:Q_SUB, :LANES], r[Q_SUB:, :LANES])
            den = jnp.where(even_lanes, r[:Q_SUB, LANES:], r[Q_SUB:, LANES:])
            mx = jnp.where(even_lanes, m[:Q_SUB], m[Q_SUB:])
            o_ref[0, q_rows, lanes] = acc / den
            lse_ref[0, q_rows, lanes] = mx + jnp.log(den)


def _attn_rows(sub_len):
    return min(sub_len, 4 * Q_SUB)


def _attn_branch(q, k, v, dilation):
    batch, seq, _ = q.shape
    sub_len = seq // dilation
    rows = _attn_rows(sub_len)
    halo_per_tile = rows // HALF
    last_halo = sub_len // HALF - 1
    view = lambda a: a.reshape(batch, sub_len, dilation * ATTN_WIDTH)
    main = pl.BlockSpec((1, rows, ATTN_WIDTH), lambda b, r, j: (b, j, r))
    prev = pl.BlockSpec((1, HALF, ATTN_WIDTH), lambda b, r, j: (b, jnp.maximum(j * halo_per_tile - 1, 0), r))
    nxt = pl.BlockSpec((1, HALF, ATTN_WIDTH),
                       lambda b, r, j: (b, jnp.minimum((j + 1) * halo_per_tile, last_halo), r))
    out_shape = jax.ShapeDtypeStruct((batch, sub_len, dilation * ATTN_WIDTH), _F32)
    o, lse = pl.pallas_call(
        functools.partial(_attn_kernel, dilation=dilation, rows=rows),
        name=f"attn_dilation{dilation}",
        grid=(batch, dilation, sub_len // rows),
        in_specs=[main, prev, main, nxt, prev, main, nxt],
        out_specs=[main, main],
        out_shape=[out_shape, out_shape],
        scratch_shapes=[pltpu.VMEM((HEAD_PAIRS, 2 * Q_SUB, K_WIN), _F32)],
        compiler_params=pltpu.CompilerParams(dimension_semantics=("arbitrary",) * 3,
                                             vmem_limit_bytes=VMEM_LIMIT_BYTES),
    )(view(q), view(k), view(k), view(k), view(v), view(v), view(v))
    return o.reshape(batch, seq, ATTN_WIDTH), lse.reshape(batch, seq, ATTN_WIDTH)


def _mixer_tail_kernel(x1_ref, o1, o2, o3, l1, l2, l3, gp_ref, g_ref, gn_ref,
                       conv_w, conv_b, ln_g, ln_b, w_out, post_g, x2_ref, g_scr, m_scr, *, rows):
    i = pl.program_id(1)

    lse = (l1[0], l2[0], l3[0])
    top = jnp.maximum(jnp.maximum(lse[0], lse[1]), lse[2])
    w = [jnp.exp(l - top) for l in lse]
    attn = (w[0] * o1[0] + w[1] * o2[0] + w[2] * o3[0]) / (w[0] + w[1] + w[2])
    m_scr[:, :ATTN_WIDTH] = attn.astype(_BF16)

    g_scr[:CONV_HALO] = jnp.where(i > 0, gp_ref[0], 0.0)
    g_scr[CONV_HALO:CONV_HALO + rows] = g_ref[0]
    g_scr[CONV_HALO + rows:] = jnp.where(i < pl.num_programs(1) - 1, gn_ref[0], 0.0)

    for c in range(rows // CONV_ROWS):
        base = c * CONV_ROWS
        acc = jnp.broadcast_to(conv_b[...], (CONV_ROWS, CONV_WIDTH))
        for t in range(CONV_KERNEL):
            start = base + CONV_HALO - CONV_PAD + t
            acc = acc + conv_w[t:t + 1, :] * g_scr[start:start + CONV_ROWS, :]
        mu = jnp.mean(acc, axis=-1, keepdims=True)
        cen = acc - mu
        var = jnp.mean(cen * cen, axis=-1, keepdims=True)
        y = cen * lax.rsqrt(var + LN_EPS) * ln_g[...] + ln_b[...]
        m_scr[base:base + CONV_ROWS, ATTN_WIDTH:] = (y * jax.nn.sigmoid(y)).astype(_BF16)

    y = jnp.dot(m_scr[...], w_out[...], preferred_element_type=_F32)
    x2_ref[0] = x1_ref[0] + _rms(y, post_g[...])


def _mixer_tail(x1, branch_outs, glu, conv_w, conv_b, ln_g, ln_b, w_out, post_g):
    batch, seq, _ = x1.shape
    rows = 512
    halo_per_tile = rows // CONV_HALO
    last_halo = seq // CONV_HALO - 1
    tile = lambda w: pl.BlockSpec((1, rows, w), lambda b, i: (b, i, 0))
    prev = pl.BlockSpec((1, CONV_HALO, CONV_WIDTH), lambda b, i: (b, jnp.maximum(i * halo_per_tile - 1, 0), 0))
    nxt = pl.BlockSpec((1, CONV_HALO, CONV_WIDTH),
                       lambda b, i: (b, jnp.minimum((i + 1) * halo_per_tile, last_halo), 0))
    outs = [o for o, _ in branch_outs]
    lses = [l for _, l in branch_outs]
    weights = (conv_w, conv_b, ln_g, ln_b, w_out, post_g)
    return pl.pallas_call(
        functools.partial(_mixer_tail_kernel, rows=rows),
        name="mixer_tail",
        grid=(batch, seq // rows),
        in_specs=[tile(D_MODEL)] + [tile(ATTN_WIDTH)] * 6 + [prev, tile(CONV_WIDTH), nxt]
                 + [_whole_vmem() for _ in weights],
        out_specs=tile(D_MODEL),
        out_shape=jax.ShapeDtypeStruct((batch, seq, D_MODEL), _F32),
        scratch_shapes=[pltpu.VMEM((rows + 2 * CONV_HALO, CONV_WIDTH), _F32),
                        pltpu.VMEM((rows, ATTN_WIDTH + CONV_WIDTH), _BF16)],
        compiler_params=pltpu.CompilerParams(dimension_semantics=("arbitrary",) * 2,
                                             vmem_limit_bytes=VMEM_LIMIT_BYTES),
    )(x1, *outs, *lses, glu, glu, glu, *weights)


def _ffn_weights(pre_g, w_gu, w_down, post_g):
    chunked = lambda w: w.reshape(D_MODEL, N_FF_CHUNKS, FF_CHUNK).transpose(1, 0, 2).astype(_BF16)
    return (pre_g.reshape(1, D_MODEL), chunked(w_gu[:, :D_FF]), chunked(w_gu[:, D_FF:]),
            w_down.astype(_BF16), post_g.reshape(1, D_MODEL))


def _layer(x, ffn1, mix_g, w_in_parts, conv_w, conv_b, ln_g, ln_b, w_out, mix_post_g, ffn2, final_g):
    batch, seq, _ = x.shape
    tokens = batch * seq
    x1, q, k, v, glu = _ffn_call(
        _ffn_proj_kernel, x.reshape(tokens, D_MODEL), ffn1 + (mix_g,) + w_in_parts,
        [(D_MODEL, _F32), (ATTN_WIDTH, _BF16), (ATTN_WIDTH, _BF16), (ATTN_WIDTH, _BF16), (CONV_WIDTH, _F32)])
    seq_view = lambda a: a.reshape(batch, seq, a.shape[-1])
    q, k, v = seq_view(q), seq_view(k), seq_view(v)
    branch_outs = [_attn_branch(q, k, v, dilation) for _, dilation in DILATED_BRANCHES]
    x2 = _mixer_tail(seq_view(x1), branch_outs, seq_view(glu), conv_w, conv_b, ln_g, ln_b, w_out, mix_post_g)
    (y,) = _ffn_call(_ffn_final_kernel, x2.reshape(tokens, D_MODEL), ffn2 + (final_g,), [(D_MODEL, _F32)])
    return y.reshape(batch, seq, D_MODEL)


def _layer_weights(l, ffn1_pre_g, ffn1_w_gu, ffn1_w_down, ffn1_post_g, mix_pre_g, w_in, conv_w, conv_b,
                   conv_ln_g, conv_ln_b, w_out, mix_post_g, ffn2_pre_g, ffn2_w_gu, ffn2_w_down, ffn2_post_g,
                   final_g):
    w_in_l = w_in[l].astype(_BF16)
    splits = (0, ATTN_WIDTH, 2 * ATTN_WIDTH, 3 * ATTN_WIDTH, 3 * ATTN_WIDTH + CONV_WIDTH,
              3 * ATTN_WIDTH + 2 * CONV_WIDTH)
    w_in_parts = tuple(w_in_l[:, a:b] for a, b in zip(splits[:-1], splits[1:]))
    row = lambda g, n: g[l].reshape(1, n)
    return (_ffn_weights(ffn1_pre_g[l], ffn1_w_gu[l], ffn1_w_down[l], ffn1_post_g[l]),
            row(mix_pre_g, D_MODEL), w_in_parts,
            conv_w[l].reshape(CONV_KERNEL, CONV_WIDTH), row(conv_b, CONV_WIDTH),
            row(conv_ln_g, CONV_WIDTH), row(conv_ln_b, CONV_WIDTH),
            w_out[l].astype(_BF16), row(mix_post_g, D_MODEL),
            _ffn_weights(ffn2_pre_g[l], ffn2_w_gu[l], ffn2_w_down[l], ffn2_post_g[l]),
            row(final_g, D_MODEL))


def kernel(x_prompt, x_sample, ffn1_pre_g, ffn1_w_gu, ffn1_w_down, ffn1_post_g, mix_pre_g, w_in, conv_w, conv_b, conv_ln_g, conv_ln_b, w_out, mix_post_g, ffn2_pre_g, ffn2_w_gu, ffn2_w_down, ffn2_post_g, final_g):
    params = (ffn1_pre_g, ffn1_w_gu, ffn1_w_down, ffn1_post_g, mix_pre_g, w_in, conv_w, conv_b, conv_ln_g,
              conv_ln_b, w_out, mix_post_g, ffn2_pre_g, ffn2_w_gu, ffn2_w_down, ffn2_post_g, final_g)
    layers = [_layer_weights(l, *params) for l in range(ffn1_pre_g.shape[0])]
    outs = []
    for x in (x_prompt, x_sample):
        for layer in layers:
            x = _layer(x, *layer)
        outs.append(x)
    return tuple(outs)
```

```python
import functools

import jax
import jax.numpy as jnp
from jax import lax
from jax.experimental import pallas as pl
from jax.experimental.pallas import tpu as pltpu

D_MODEL = 1024
D_FF = 2816
ATTN_WIDTH = 512
CONV_WIDTH = 512
HEAD_DIM = 64
N_HEADS = ATTN_WIDTH // HEAD_DIM
HEAD_PAIRS = N_HEADS // 2
DILATED_BRANCHES = ((128, 1), (512, 4), (2048, 16))
DILATIONS = tuple(d for _, d in DILATED_BRANCHES)
HALF = 64
CONV_KERNEL = 31
CONV_PAD = CONV_KERNEL // 2
FFN_RESIDUAL_WEIGHT = 0.5
RMS_EPS = 1e-6
LN_EPS = 1e-5
NEG_INF = -1e30

LANES = 128
SUBLANES = 8
LANE_CHUNKS = ATTN_WIDTH // LANES
Q_SUB = 2 * HALF
K_WIN = Q_SUB + 2 * HALF
FF_CHUNK = 256
N_FF_CHUNKS = D_FF // FF_CHUNK
FFN_TILE = 512
TAIL_TILE = 512
CONV_HALO = 16
CONV_ROWS = 32
CONV_FIRST = CONV_HALO - CONV_PAD
CONV_SPAN = -(-(CONV_FIRST + CONV_KERNEL - 1) // SUBLANES) * SUBLANES
VMEM_LIMIT_BYTES = 56 * 1024 * 1024

assert all(w // (2 * d) == HALF for w, d in DILATED_BRANCHES)
assert DILATIONS[0] == 1 and D_FF % FF_CHUNK == 0

_BF16 = jnp.bfloat16
_F32 = jnp.float32


def _rms(x, g):
    ms = jnp.mean(x * x, axis=-1, keepdims=True)
    return x * lax.rsqrt(ms + RMS_EPS) * g


def _whole_vmem():
    return pl.BlockSpec(memory_space=pltpu.VMEM)


def _ffn_body(x_ref, pre_g, wg, wu, wd, post_g, h_scr, a_scr):
    h_scr[...] = _rms(x_ref[...], pre_g[...]).astype(_BF16)
    for c in range(N_FF_CHUNKS):
        h = h_scr[...]
        gate = jnp.dot(h, wg[c], preferred_element_type=_F32)
        up = jnp.dot(h, wu[c], preferred_element_type=_F32)
        a_scr[:, c * FF_CHUNK:(c + 1) * FF_CHUNK] = (gate * jax.nn.sigmoid(gate) * up).astype(_BF16)
    y = jnp.dot(a_scr[...], wd[...], preferred_element_type=_F32)
    return x_ref[...] + FFN_RESIDUAL_WEIGHT * _rms(y, post_g[...])


def _store_views(proj, view_refs, lane_scr):
    rows = proj.shape[0]
    view_refs[0][...] = proj.astype(_BF16)
    for c in range(LANE_CHUNKS):
        lane_scr[c] = proj[:, c * LANES:(c + 1) * LANES]
    for dilation, ref in zip(DILATIONS[1:], view_refs[1:]):
        for r in range(dilation):
            for c in range(LANE_CHUNKS):
                col = r * ATTN_WIDTH + c * LANES
                ref[:, col:col + LANES] = lane_scr[c, pl.ds(r, rows // dilation, stride=dilation), :].astype(_BF16)


def _ffn_proj_kernel(x_ref, pre_g, wg, wu, wd, post_g, mix_g, wq, wk, wv, wa, wb,
                     x1_ref, q1_ref, q4_ref, q16_ref, k1_ref, k4_ref, k16_ref, v1_ref, v4_ref, v16_ref, glu_ref,
                     h_scr, a_scr, lane_scr):
    x1 = _ffn_body(x_ref, pre_g, wg, wu, wd, post_g, h_scr, a_scr)
    x1_ref[...] = x1
    h_scr[...] = _rms(x1, mix_g[...]).astype(_BF16)
    q = jnp.dot(h_scr[...], wq[...], preferred_element_type=_F32) * (HEAD_DIM ** -0.5)
    _store_views(q, (q1_ref, q4_ref, q16_ref), lane_scr)
    k = jnp.dot(h_scr[...], wk[...], preferred_element_type=_F32)
    _store_views(k, (k1_ref, k4_ref, k16_ref), lane_scr)
    v = jnp.dot(h_scr[...], wv[...], preferred_element_type=_F32)
    _store_views(v, (v1_ref, v4_ref, v16_ref), lane_scr)
    ga = jnp.dot(h_scr[...], wa[...], preferred_element_type=_F32)
    gb = jnp.dot(h_scr[...], wb[...], preferred_element_type=_F32)
    glu_ref[...] = ga * jax.nn.sigmoid(gb)


def _ffn_final_kernel(x_ref, pre_g, wg, wu, wd, post_g, final_g, y_ref, h_scr, a_scr):
    x1 = _ffn_body(x_ref, pre_g, wg, wu, wd, post_g, h_scr, a_scr)
    y_ref[...] = _rms(x1, final_g[...])


def _ffn_call(body, x, weights, outs, extra_scratch=()):
    n_tokens = x.shape[0]
    tm = FFN_TILE
    row_spec = lambda d, w: pl.BlockSpec((tm // d, d * w), lambda i: (i, 0))
    return pl.pallas_call(
        body,
        name=body.__name__.strip("_"),
        grid=(n_tokens // tm,),
        in_specs=[row_spec(1, D_MODEL)] + [_whole_vmem() for _ in weights],
        out_specs=[row_spec(d, w) for d, w, _ in outs],
        out_shape=[jax.ShapeDtypeStruct((n_tokens // d, d * w), dt) for d, w, dt in outs],
        scratch_shapes=[pltpu.VMEM((tm, D_MODEL), _BF16), pltpu.VMEM((tm, D_FF), _BF16), *extra_scratch],
        compiler_params=pltpu.CompilerParams(dimension_semantics=("arbitrary",),
                                             vmem_limit_bytes=VMEM_LIMIT_BYTES),
    )(x, *weights)


def _attn_kernel(q_ref, kp_ref, k_ref, kn_ref, vp_ref, v_ref, vn_ref, o_ref, lse_ref, bias_scr,
                 *, dilation, rows):
    j = pl.program_id(2)
    n_sub = rows // Q_SUB

    @pl.when((pl.program_id(0) == 0) & (pl.program_id(1) == 0) & (j == 0))
    def _():
        row = lax.broadcasted_iota(jnp.int32, (2 * Q_SUB, K_WIN), 0)
        col = lax.broadcasted_iota(jnp.int32, (2 * Q_SUB, K_WIN), 1)
        rel = jnp.abs(col - HALF - (row & (Q_SUB - 1)))
        dist = rel.astype(_F32)
        for p in range(HEAD_PAIRS):
            slope_even = dilation * 2.0 ** (-8.0 * (2 * p + 1) / N_HEADS)
            slope_odd = dilation * 2.0 ** (-8.0 * (2 * p + 2) / N_HEADS)
            slope = jnp.where(row < Q_SUB, slope_even, slope_odd)
            bias_scr[p] = jnp.where(rel <= HALF, -slope * dist, NEG_INF)

    lane = lax.broadcasted_iota(jnp.int32, (Q_SUB, LANES), 1)
    even_lanes = lane < HEAD_DIM
    col = lax.broadcasted_iota(jnp.int32, (2 * Q_SUB, K_WIN), 1)
    first_valid_col = jnp.where(j == 0, HALF, 0)
    last_valid_col = jnp.where(j == pl.num_programs(2) - 1, K_WIN - HALF, K_WIN)
    ones = jnp.ones((K_WIN, LANES), _BF16)

    def window(prev_ref, main_ref, next_ref, i, lanes):
        parts = []
        lo = i * Q_SUB - HALF
        hi = lo + K_WIN
        if lo < 0:
            parts.append(prev_ref[0, :, lanes])
            lo = 0
        parts.append(main_ref[0, lo:min(hi, rows), lanes])
        if hi > rows:
            parts.append(next_ref[0, :, lanes])
        return parts[0] if len(parts) == 1 else jnp.concatenate(parts, axis=0)

    for i in range(n_sub):
        q_rows = slice(i * Q_SUB, (i + 1) * Q_SUB)
        for p in range(HEAD_PAIRS):
            lanes = slice(p * LANES, (p + 1) * LANES)
            q2 = q_ref[0, q_rows, lanes]
            zero = jnp.zeros_like(q2)
            q_stack = jnp.concatenate([jnp.where(even_lanes, q2, zero), jnp.where(even_lanes, zero, q2)], axis=0)
            kw = window(kp_ref, k_ref, kn_ref, i, lanes)
            s = lax.dot_general(q_stack, kw, (((1,), (1,)), ((), ())), preferred_element_type=_F32)
            s = s + bias_scr[p]
            if i == 0:
                s = jnp.where(col < first_valid_col, NEG_INF, s)
            if i == n_sub - 1:
                s = jnp.where(col >= last_valid_col, NEG_INF, s)
            m = jnp.max(s, axis=-1, keepdims=True)
            e = jnp.exp(s - m).astype(_BF16)
            vw = window(vp_ref, v_ref, vn_ref, i, lanes)
            r = jnp.dot(e, jnp.concatenate([vw, ones], axis=1), preferred_element_type=_F32)
            acc = jnp.where(even_lanes, r[:Q_SUB, :LANES], r[Q_SUB:, :LANES])
            den = jnp.where(even_lanes, r[:Q_SUB, LANES:], r[Q_SUB:, LANES:])
            mx = jnp.where(even_lanes, m[:Q_SUB], m[Q_SUB:])
            o_ref[0, q_rows, lanes] = acc / den
            lse_ref[0, q_rows, lanes] = mx + jnp.log(den)


def _attn_rows(sub_len):
    return min(sub_len, 4 * Q_SUB)


def _attn_branch(q, k, v, dilation):
    batch, sub_len, _ = q.shape
    rows = _attn_rows(sub_len)
    halo_per_tile = rows // HALF
    last_halo = sub_len // HALF - 1
    main = pl.BlockSpec((1, rows, ATTN_WIDTH), lambda b, r, j: (b, j, r))
    prev = pl.BlockSpec((1, HALF, ATTN_WIDTH), lambda b, r, j: (b, jnp.maximum(j * halo_per_tile - 1, 0), r))
    nxt = pl.BlockSpec((1, HALF, ATTN_WIDTH),
                       lambda b, r, j: (b, jnp.minimum((j + 1) * halo_per_tile, last_halo), r))
    out_shape = jax.ShapeDtypeStruct((batch, sub_len, dilation * ATTN_WIDTH), _F32)
    return pl.pallas_call(
        functools.partial(_attn_kernel, dilation=dilation, rows=rows),
        name=f"attn_dilation{dilation}",
        grid=(batch, dilation, sub_len // rows),
        in_specs=[main, prev, main, nxt, prev, main, nxt],
        out_specs=[main, main],
        out_shape=[out_shape, out_shape],
        scratch_shapes=[pltpu.VMEM((HEAD_PAIRS, 2 * Q_SUB, K_WIN), _F32)],
        compiler_params=pltpu.CompilerParams(dimension_semantics=("arbitrary",) * 3,
                                             vmem_limit_bytes=VMEM_LIMIT_BYTES),
    )(q, k, k, k, v, v, v)


def _mixer_tail_kernel(x1_ref, o1, o4, o16, l1, l4, l16, gp_ref, g_ref, gn_ref,
                       conv_w, conv_b, ln_g, ln_b, w_out, post_g, x2_ref,
                       nat_scr, g_scr, shift_scr, m_scr, *, rows):
    i = pl.program_id(1)

    for a, ref in enumerate((o4, l4, o16, l16)):
        dilation = DILATIONS[1 + a // 2]
        for r in range(dilation):
            for c in range(LANE_CHUNKS):
                col = r * ATTN_WIDTH + c * LANES
                nat_scr[a, c, pl.ds(r, rows // dilation, stride=dilation), :] = ref[0, :, col:col + LANES]

    for c in range(LANE_CHUNKS):
        lanes = slice(c * LANES, (c + 1) * LANES)
        outs = (o1[0, :, lanes], nat_scr[0, c], nat_scr[2, c])
        lses = (l1[0, :, lanes], nat_scr[1, c], nat_scr[3, c])
        top = jnp.maximum(jnp.maximum(lses[0], lses[1]), lses[2])
        w = [jnp.exp(l - top) for l in lses]
        attn = (w[0] * outs[0] + w[1] * outs[1] + w[2] * outs[2]) / (w[0] + w[1] + w[2])
        m_scr[:, lanes] = attn.astype(_BF16)

    g_scr[:CONV_HALO] = jnp.where(i > 0, gp_ref[0], 0.0)
    g_scr[CONV_HALO:CONV_HALO + rows] = g_ref[0]
    g_scr[CONV_HALO + rows:CONV_HALO + rows + CONV_HALO] = jnp.where(i < pl.num_programs(1) - 1, gn_ref[0], 0.0)
    for b in range(SUBLANES):
        shift_scr[b] = g_scr[b:b + rows + CONV_SPAN - SUBLANES, :]

    for c in range(rows // CONV_ROWS):
        base = c * CONV_ROWS
        acc = jnp.broadcast_to(conv_b[...], (CONV_ROWS, CONV_WIDTH))
        for t in range(CONV_KERNEL):
            off = CONV_FIRST + t
            start = base + off // SUBLANES * SUBLANES
            acc = acc + conv_w[t:t + 1, :] * shift_scr[off % SUBLANES, start:start + CONV_ROWS, :]
        mu = jnp.mean(acc, axis=-1, keepdims=True)
        cen = acc - mu
        var = jnp.mean(cen * cen, axis=-1, keepdims=True)
        y = cen * lax.rsqrt(var + LN_EPS) * ln_g[...] + ln_b[...]
        m_scr[base:base + CONV_ROWS, ATTN_WIDTH:] = (y * jax.nn.sigmoid(y)).astype(_BF16)

    y = jnp.dot(m_scr[...], w_out[...], preferred_element_type=_F32)
    x2_ref[0] = x1_ref[0] + _rms(y, post_g[...])


def _mixer_tail(x1, branch_outs, glu, conv_w, conv_b, ln_g, ln_b, w_out, post_g):
    batch, seq, _ = x1.shape
    rows = TAIL_TILE
    halo_per_tile = rows // CONV_HALO
    last_halo = seq // CONV_HALO - 1
    tile = lambda w, d=1: pl.BlockSpec((1, rows // d, d * w), lambda b, i: (b, i, 0))
    prev = pl.BlockSpec((1, CONV_HALO, CONV_WIDTH), lambda b, i: (b, jnp.maximum(i * halo_per_tile - 1, 0), 0))
    nxt = pl.BlockSpec((1, CONV_HALO, CONV_WIDTH),
                       lambda b, i: (b, jnp.minimum((i + 1) * halo_per_tile, last_halo), 0))
    outs = [o for o, _ in branch_outs]
    lses = [l for _, l in branch_outs]
    views = [tile(ATTN_WIDTH, d) for d in DILATIONS]
    weights = (conv_w, conv_b, ln_g, ln_b, w_out, post_g)
    return pl.pallas_call(
        functools.partial(_mixer_tail_kernel, rows=rows),
        name="mixer_tail",
        grid=(batch, seq // rows),
        in_specs=[tile(D_MODEL)] + views + views + [prev, tile(CONV_WIDTH), nxt]
                 + [_whole_vmem() for _ in weights],
        out_specs=tile(D_MODEL),
        out_shape=jax.ShapeDtypeStruct((batch, seq, D_MODEL), _F32),
        scratch_shapes=[pltpu.VMEM((4, LANE_CHUNKS, rows, LANES), _F32),
                        pltpu.VMEM((rows + CONV_SPAN, CONV_WIDTH), _F32),
                        pltpu.VMEM((SUBLANES, rows + CONV_SPAN - SUBLANES, CONV_WIDTH), _F32),
                        pltpu.VMEM((rows, ATTN_WIDTH + CONV_WIDTH), _BF16)],
        compiler_params=pltpu.CompilerParams(dimension_semantics=("arbitrary",) * 2,
                                             vmem_limit_bytes=VMEM_LIMIT_BYTES),
    )(x1, *outs, *lses, glu, glu, glu, *weights)


def _ffn_weights(pre_g, w_gu, w_down, post_g):
    chunked = lambda w: w.reshape(D_MODEL, N_FF_CHUNKS, FF_CHUNK).transpose(1, 0, 2).astype(_BF16)
    return (pre_g.reshape(1, D_MODEL), chunked(w_gu[:, :D_FF]), chunked(w_gu[:, D_FF:]),
            w_down.astype(_BF16), post_g.reshape(1, D_MODEL))


def _layer(x, ffn1, mix_g, w_in_parts, conv_w, conv_b, ln_g, ln_b, w_out, mix_post_g, ffn2, final_g):
    batch, seq, _ = x.shape
    tokens = batch * seq
    qkv_views = [(d, ATTN_WIDTH, _BF16) for _ in range(3) for d in DILATIONS]
    x1, *qkv, glu = _ffn_call(
        _ffn_proj_kernel, x.reshape(tokens, D_MODEL), ffn1 + (mix_g,) + w_in_parts,
        [(1, D_MODEL, _F32)] + qkv_views + [(1, CONV_WIDTH, _F32)],
        extra_scratch=[pltpu.VMEM((LANE_CHUNKS, FFN_TILE, LANES), _F32)])
    per_batch = lambda a: a.reshape(batch, a.shape[0] // batch, a.shape[1])
    n = len(DILATIONS)
    branch_outs = [_attn_branch(per_batch(qkv[b]), per_batch(qkv[n + b]), per_batch(qkv[2 * n + b]), d)
                   for b, d in enumerate(DILATIONS)]
    x2 = _mixer_tail(per_batch(x1), branch_outs, per_batch(glu), conv_w, conv_b, ln_g, ln_b, w_out, mix_post_g)
    (y,) = _ffn_call(_ffn_final_kernel, x2.reshape(tokens, D_MODEL), ffn2 + (final_g,), [(1, D_MODEL, _F32)])
    return y.reshape(batch, seq, D_MODEL)


def _layer_weights(l, ffn1_pre_g, ffn1_w_gu, ffn1_w_down, ffn1_post_g, mix_pre_g, w_in, conv_w, conv_b,
                   conv_ln_g, conv_ln_b, w_out, mix_post_g, ffn2_pre_g, ffn2_w_gu, ffn2_w_down, ffn2_post_g,
                   final_g):
    w_in_l = w_in[l].astype(_BF16)
    splits = (0, ATTN_WIDTH, 2 * ATTN_WIDTH, 3 * ATTN_WIDTH, 3 * ATTN_WIDTH + CONV_WIDTH,
              3 * ATTN_WIDTH + 2 * CONV_WIDTH)
    w_in_parts = tuple(w_in_l[:, a:b] for a, b in zip(splits[:-1], splits[1:]))
    row = lambda g, n: g[l].reshape(1, n)
    return (_ffn_weights(ffn1_pre_g[l], ffn1_w_gu[l], ffn1_w_down[l], ffn1_post_g[l]),
            row(mix_pre_g, D_MODEL), w_in_parts,
            conv_w[l].reshape(CONV_KERNEL, CONV_WIDTH), row(conv_b, CONV_WIDTH),
            row(conv_ln_g, CONV_WIDTH), row(conv_ln_b, CONV_WIDTH),
            w_out[l].astype(_BF16), row(mix_post_g, D_MODEL),
            _ffn_weights(ffn2_pre_g[l], ffn2_w_gu[l], ffn2_w_down[l], ffn2_post_g[l]),
            row(final_g, D_MODEL))


def kernel(x_prompt, x_sample, ffn1_pre_g, ffn1_w_gu, ffn1_w_down, ffn1_post_g, mix_pre_g, w_in, conv_w, conv_b, conv_ln_g, conv_ln_b, w_out, mix_post_g, ffn2_pre_g, ffn2_w_gu, ffn2_w_down, ffn2_post_g, final_g):
    params = (ffn1_pre_g, ffn1_w_gu, ffn1_w_down, ffn1_post_g, mix_pre_g, w_in, conv_w, conv_b, conv_ln_g,
              conv_ln_b, w_out, mix_post_g, ffn2_pre_g, ffn2_w_gu, ffn2_w_down, ffn2_post_g, final_g)
    layers = [_layer_weights(l, *params) for l in range(ffn1_pre_g.shape[0])]
    outs = []
    for x in (x_prompt, x_sample):
        for layer in layers:
            x = _layer(x, *layer)
        outs.append(x)
    return tuple(outs)
```

```python
import functools

import jax
import jax.numpy as jnp
from jax import lax
from jax.experimental import pallas as pl
from jax.experimental.pallas import tpu as pltpu

D_MODEL = 1024
D_FF = 2816
ATTN_WIDTH = 512
CONV_WIDTH = 512
HEAD_DIM = 64
N_HEADS = ATTN_WIDTH // HEAD_DIM
HEAD_PAIRS = N_HEADS // 2
DILATED_BRANCHES = ((128, 1), (512, 4), (2048, 16))
DILATIONS = tuple(d for _, d in DILATED_BRANCHES)
HALF = 64
CONV_KERNEL = 31
CONV_PAD = CONV_KERNEL // 2
FFN_RESIDUAL_WEIGHT = 0.5
RMS_EPS = 1e-6
LN_EPS = 1e-5
NEG_INF = -1e30

LANES = 128
SUBLANES = 8
LANE_CHUNKS = ATTN_WIDTH // LANES
Q_SUB = 2 * HALF
K_WIN = Q_SUB + 2 * HALF
FF_CHUNK = 256
N_FF_CHUNKS = D_FF // FF_CHUNK
FFN_TILE = 512
TAIL_TILE = 512
CONV_HALO = 16
CONV_TILE = 64
CONV_ROWS = 32
CONV_FIRST = CONV_HALO - CONV_PAD
CONV_SPAN = -(-(CONV_FIRST + CONV_KERNEL - 1) // SUBLANES) * SUBLANES
VMEM_LIMIT_BYTES = 62 * 1024 * 1024

assert all(w // (2 * d) == HALF for w, d in DILATED_BRANCHES)
assert DILATIONS[0] == 1 and D_FF % FF_CHUNK == 0

_BF16 = jnp.bfloat16
_F32 = jnp.float32


def _rms(x, g):
    ms = jnp.mean(x * x, axis=-1, keepdims=True)
    return x * lax.rsqrt(ms + RMS_EPS) * g


def _zero_token(x):
    bits = lax.bitcast_convert_type(x, jnp.uint32)
    return lax.shift_right_logical(lax.shift_right_logical(bits, jnp.uint32(16)), jnp.uint32(16))


def _whole_vmem():
    return pl.BlockSpec(memory_space=pltpu.VMEM)


def _swiglu_chunk(c, h_scr, a_scr, wg, wu):
    h = h_scr[...]
    gate = jnp.dot(h, wg[c], preferred_element_type=_F32)
    up = jnp.dot(h, wu[c], preferred_element_type=_F32)
    a_scr[:, c * FF_CHUNK:(c + 1) * FF_CHUNK] = (gate * jax.nn.sigmoid(gate) * up).astype(_BF16)


def _swiglu(h_scr, a_scr, wg, wu, wd):
    for c in range(N_FF_CHUNKS):
        _swiglu_chunk(c, h_scr, a_scr, wg, wu)
    return jnp.dot(a_scr[...], wd[...], preferred_element_type=_F32)


def _store_views(proj, view_refs, lane_scr):
    rows = proj.shape[0]
    view_refs[0][...] = proj.astype(_BF16)
    for c in range(LANE_CHUNKS):
        lane_scr[c] = proj[:, c * LANES:(c + 1) * LANES]
    for dilation, ref in zip(DILATIONS[1:], view_refs[1:]):
        for r in range(dilation):
            for c in range(LANE_CHUNKS):
                col = r * ATTN_WIDTH + c * LANES
                ref[:, col:col + LANES] = lane_scr[c, pl.ds(r, rows // dilation, stride=dilation), :].astype(_BF16)


def _ffn_proj_kernel(x_ref, pre_g, wg, wu, wd, post_g, mix_g, wq, wk, wv, wa, wb,
                     x1_ref, q1_ref, q4_ref, q16_ref, k1_ref, k4_ref, k16_ref, v1_ref, v4_ref, v16_ref, glu_ref,
                     h_scr, a_scr, lane_scr):
    h_scr[...] = _rms(x_ref[...], pre_g[...]).astype(_BF16)
    x1 = x_ref[...] + FFN_RESIDUAL_WEIGHT * _rms(_swiglu(h_scr, a_scr, wg, wu, wd), post_g[...])
    x1_ref[...] = x1
    h_scr[...] = _rms(x1, mix_g[...]).astype(_BF16)
    q = jnp.dot(h_scr[...], wq[...], preferred_element_type=_F32) * (HEAD_DIM ** -0.5)
    _store_views(q, (q1_ref, q4_ref, q16_ref), lane_scr)
    k = jnp.dot(h_scr[...], wk[...], preferred_element_type=_F32)
    _store_views(k, (k1_ref, k4_ref, k16_ref), lane_scr)
    v = jnp.dot(h_scr[...], wv[...], preferred_element_type=_F32)
    _store_views(v, (v1_ref, v4_ref, v16_ref), lane_scr)
    ga = jnp.dot(h_scr[...], wa[...], preferred_element_type=_F32)
    gb = jnp.dot(h_scr[...], wb[...], preferred_element_type=_F32)
    glu_ref[...] = ga * jax.nn.sigmoid(gb)


def _ffn_proj(x, weights):
    n_tokens = x.shape[0]
    tm = FFN_TILE
    row_spec = lambda d, w: pl.BlockSpec((tm // d, d * w), lambda i: (i, 0))
    outs = ([(1, D_MODEL, _F32)] + [(d, ATTN_WIDTH, _BF16) for _ in range(3) for d in DILATIONS]
            + [(1, CONV_WIDTH, _F32)])
    return pl.pallas_call(
        _ffn_proj_kernel,
        name="ffn_proj",
        grid=(n_tokens // tm,),
        in_specs=[row_spec(1, D_MODEL)] + [_whole_vmem() for _ in weights],
        out_specs=[row_spec(d, w) for d, w, _ in outs],
        out_shape=[jax.ShapeDtypeStruct((n_tokens // d, d * w), dt) for d, w, dt in outs],
        scratch_shapes=[pltpu.VMEM((tm, D_MODEL), _BF16), pltpu.VMEM((tm, D_FF), _BF16),
                        pltpu.VMEM((LANE_CHUNKS, tm, LANES), _F32)],
        compiler_params=pltpu.CompilerParams(dimension_semantics=("arbitrary",),
                                             vmem_limit_bytes=VMEM_LIMIT_BYTES),
    )(x, *weights)


def _attn_kernel(q_ref, kp_ref, k_ref, kn_ref, vp_ref, v_ref, vn_ref, o_ref, lse_ref, bias_scr,
                 *, dilation, rows):
    j = pl.program_id(2)
    n_sub = rows // Q_SUB

    @pl.when((pl.program_id(0) == 0) & (pl.program_id(1) == 0) & (j == 0))
    def _():
        row = lax.broadcasted_iota(jnp.int32, (2 * Q_SUB, K_WIN), 0)
        col = lax.broadcasted_iota(jnp.int32, (2 * Q_SUB, K_WIN), 1)
        rel = jnp.abs(col - HALF - (row & (Q_SUB - 1)))
        dist = rel.astype(_F32)
        for p in range(HEAD_PAIRS):
            slope_even = dilation * 2.0 ** (-8.0 * (2 * p + 1) / N_HEADS)
            slope_odd = dilation * 2.0 ** (-8.0 * (2 * p + 2) / N_HEADS)
            slope = jnp.where(row < Q_SUB, slope_even, slope_odd)
            bias_scr[p] = jnp.where(rel <= HALF, -slope * dist, NEG_INF)

    lane = lax.broadcasted_iota(jnp.int32, (Q_SUB, LANES), 1)
    even_lanes = lane < HEAD_DIM
    col = lax.broadcasted_iota(jnp.int32, (2 * Q_SUB, K_WIN), 1)
    first_valid_col = jnp.where(j == 0, HALF, 0)
    last_valid_col = jnp.where(j == pl.num_programs(2) - 1, K_WIN - HALF, K_WIN)
    ones = jnp.ones((K_WIN, LANES), _BF16)

    def window(prev_ref, main_ref, next_ref, i, lanes):
        parts = []
        lo = i * Q_SUB - HALF
        hi = lo + K_WIN
        if lo < 0:
            parts.append(prev_ref[0, :, lanes])
            lo = 0
        parts.append(main_ref[0, lo:min(hi, rows), lanes])
        if hi > rows:
            parts.append(next_ref[0, :, lanes])
        return parts[0] if len(parts) == 1 else jnp.concatenate(parts, axis=0)

    for i in range(n_sub):
        q_rows = slice(i * Q_SUB, (i + 1) * Q_SUB)
        for p in range(HEAD_PAIRS):
            lanes = slice(p * LANES, (p + 1) * LANES)
            q2 = q_ref[0, q_rows, lanes]
            zero = jnp.zeros_like(q2)
            q_stack = jnp.concatenate([jnp.where(even_lanes, q2, zero), jnp.where(even_lanes, zero, q2)], axis=0)
            kw = window(kp_ref, k_ref, kn_ref, i, lanes)
            s = lax.dot_general(q_stack, kw, (((1,), (1,)), ((), ())), preferred_element_type=_F32)
            s = s + bias_scr[p]
            if i == 0:
                s = jnp.where(col < first_valid_col, NEG_INF, s)
            if i == n_sub - 1:
                s = jnp.where(col >= last_valid_col, NEG_INF, s)
            m = jnp.max(s, axis=-1, keepdims=True)
            e = jnp.exp(s - m).astype(_BF16)
            vw = window(vp_ref, v_ref, vn_ref, i, lanes)
            r = jnp.dot(e, jnp.concatenate([vw, ones], axis=1), preferred_element_type=_F32)
            acc = jnp.where(even_lanes, r[:Q_SUB, :LANES], r[Q_SUB:, :LANES])
            den = jnp.where(even_lanes, r[:Q_SUB, LANES:], r[Q_SUB:, LANES:])
            mx = jnp.where(even_lanes, m[:Q_SUB], m[Q_SUB:])
            o_ref[0, q_rows, lanes] = acc / den
            lse_ref[0, q_rows, lanes] = mx + jnp.log(den)


def _attn_rows(sub_len):
    return min(sub_len, 4 * Q_SUB)


def _attn_branch(q, k, v, dilation):
    batch, sub_len, _ = q.shape
    rows = _attn_rows(sub_len)
    halo_per_tile = rows // HALF
    last_halo = sub_len // HALF - 1
    main = pl.BlockSpec((1, rows, ATTN_WIDTH), lambda b, r, j: (b, j, r))
    prev = pl.BlockSpec((1, HALF, ATTN_WIDTH), lambda b, r, j: (b, jnp.maximum(j * halo_per_tile - 1, 0), r))
    nxt = pl.BlockSpec((1, HALF, ATTN_WIDTH),
                       lambda b, r, j: (b, jnp.minimum((j + 1) * halo_per_tile, last_halo), r))
    out_shape = jax.ShapeDtypeStruct((batch, sub_len, dilation * ATTN_WIDTH), _F32)
    return pl.pallas_call(
        functools.partial(_attn_kernel, dilation=dilation, rows=rows),
        name=f"attn_dilation{dilation}",
        grid=(batch, dilation, sub_len // rows),
        in_specs=[main, prev, main, nxt, prev, main, nxt],
        out_specs=[main, main],
        out_shape=[out_shape, out_shape],
        scratch_shapes=[pltpu.VMEM((HEAD_PAIRS, 2 * Q_SUB, K_WIN), _F32)],
        compiler_params=pltpu.CompilerParams(dimension_semantics=("arbitrary",) * 3,
                                             vmem_limit_bytes=VMEM_LIMIT_BYTES),
    )(q, k, k, k, v, v, v)


def _tail_ffn_kernel(x1_ref, o1, o4, o16, l1, l4, l16, gp_ref, g_ref, gn_ref,
                     conv_w, conv_b, ln_g, ln_b, w_out, post_g,
                     pre_g, wg, wu, wd, ffn_post_g, final_g,
                     y_ref,
                     x2_scr, h_scr, a_scr, nat_scr, g_scr, shift_scr, m_scr,
                     *, rows, tiles_per_seq, n_tiles):
    s = pl.program_id(0)
    i = jnp.minimum(s, n_tiles - 1) % tiles_per_seq

    @pl.when(s == 0)
    def _():
        x2_scr[...] = jnp.zeros_like(x2_scr)

    h_scr[...] = _rms(x2_scr[...], pre_g[...]).astype(_BF16)

    def merge_item(c):
        def run():
            lanes = slice(c * LANES, (c + 1) * LANES)
            for a, ref in enumerate((o4, l4, o16, l16)):
                dilation = DILATIONS[1 + a // 2]
                for r in range(dilation):
                    col = r * ATTN_WIDTH + c * LANES
                    nat_scr[a, pl.ds(r, rows // dilation, stride=dilation), :] = ref[0, :, col:col + LANES]
            outs = (o1[0, :, lanes], nat_scr[0], nat_scr[2])
            lses = (l1[0, :, lanes], nat_scr[1], nat_scr[3])
            top = jnp.maximum(jnp.maximum(lses[0], lses[1]), lses[2])
            w = [jnp.exp(l - top) for l in lses]
            attn = (w[0] * outs[0] + w[1] * outs[1] + w[2] * outs[2]) / (w[0] + w[1] + w[2])
            m_scr[:, lanes] = attn.astype(_BF16)
            return attn[-SUBLANES:, :]
        return run

    def conv_input_item():
        last = jnp.where(i < tiles_per_seq - 1, gn_ref[0], 0.0)
        g_scr[:CONV_HALO] = jnp.where(i > 0, gp_ref[0], 0.0)
        g_scr[CONV_HALO:CONV_HALO + rows] = g_ref[0]
        g_scr[CONV_HALO + rows:CONV_HALO + rows + CONV_HALO] = last
        return last[-SUBLANES:, :LANES]

    def shift_item(t0):
        def run():
            for b in range(SUBLANES):
                shifted = g_scr[t0 + b:t0 + b + CONV_TILE + CONV_SPAN - SUBLANES, :]
                shift_scr[b] = shifted
            return shifted[-SUBLANES:, :LANES]
        return run

    def conv_item(t0, base):
        def run():
            acc = jnp.broadcast_to(conv_b[...], (CONV_ROWS, CONV_WIDTH))
            for t in range(CONV_KERNEL):
                off = CONV_FIRST + t
                start = base + off // SUBLANES * SUBLANES
                acc = acc + conv_w[t:t + 1, :] * shift_scr[off % SUBLANES, start:start + CONV_ROWS, :]
            mu = jnp.mean(acc, axis=-1, keepdims=True)
            cen = acc - mu
            var = jnp.mean(cen * cen, axis=-1, keepdims=True)
            y = cen * lax.rsqrt(var + LN_EPS) * ln_g[...] + ln_b[...]
            act = y * jax.nn.sigmoid(y)
            m_scr[t0 + base:t0 + base + CONV_ROWS, ATTN_WIDTH:] = act.astype(_BF16)
            return act[-SUBLANES:, :LANES]
        return run

    tail_items = [merge_item(c) for c in range(LANE_CHUNKS)] + [conv_input_item]
    for t0 in range(0, rows, CONV_TILE):
        tail_items.append(shift_item(t0))
        tail_items += [conv_item(t0, base) for base in range(0, CONV_TILE, CONV_ROWS)]

    def anchor(ref, token):
        bits = pltpu.bitcast(ref[:2 * SUBLANES, :LANES], jnp.uint32)
        ref[:2 * SUBLANES, :LANES] = pltpu.bitcast(bits | token, _BF16)

    done = 0
    for c in range(N_FF_CHUNKS):
        _swiglu_chunk(c, h_scr, a_scr, wg, wu)
        upto = len(tail_items) * (c + 1) // N_FF_CHUNKS
        token = None
        for item in tail_items[done:upto]:
            t = _zero_token(item())
            token = t if token is None else token | t
        done = upto
        if token is not None and c + 1 < N_FF_CHUNKS:
            anchor(h_scr, token)

    mixed = jnp.dot(m_scr[...], w_out[...], preferred_element_type=_F32)
    ffn = jnp.dot(a_scr[...], wd[...], preferred_element_type=_F32)

    x3 = x2_scr[...] + FFN_RESIDUAL_WEIGHT * _rms(ffn, ffn_post_g[...])
    y_ref[...] = _rms(x3, final_g[...])
    x2_scr[...] = x1_ref[0] + _rms(mixed, post_g[...])


def _tail_ffn(x1, branch_outs, glu, conv_w, conv_b, ln_g, ln_b, w_out, post_g, ffn2, final_g):
    batch, seq, _ = x1.shape
    rows = TAIL_TILE
    tiles_per_seq = seq // rows
    n_tiles = batch * tiles_per_seq
    halo_per_tile = rows // CONV_HALO
    last_halo = seq // CONV_HALO - 1

    def tail_tile(s):
        t = jnp.minimum(s, n_tiles - 1)
        return t // tiles_per_seq, t % tiles_per_seq

    def tile(w, d=1):
        return pl.BlockSpec((1, rows // d, d * w), lambda s: (*tail_tile(s), 0))

    def halo(offset):
        def index(s):
            b, i = tail_tile(s)
            return b, jnp.clip(i * halo_per_tile + offset, 0, last_halo), 0
        return pl.BlockSpec((1, CONV_HALO, CONV_WIDTH), index)

    outs = [o for o, _ in branch_outs]
    lses = [l for _, l in branch_outs]
    views = [tile(ATTN_WIDTH, d) for d in DILATIONS]
    weights = (conv_w, conv_b, ln_g, ln_b, w_out, post_g) + ffn2 + (final_g,)
    return pl.pallas_call(
        functools.partial(_tail_ffn_kernel, rows=rows, tiles_per_seq=tiles_per_seq, n_tiles=n_tiles),
        name="tail_ffn",
        grid=(n_tiles + 1,),
        in_specs=[tile(D_MODEL)] + views + views + [halo(-1), tile(CONV_WIDTH), halo(halo_per_tile)]
                 + [_whole_vmem() for _ in weights],
        out_specs=pl.BlockSpec((rows, D_MODEL), lambda s: (jnp.maximum(s - 1, 0), 0)),
        out_shape=jax.ShapeDtypeStruct((batch * seq, D_MODEL), _F32),
        scratch_shapes=[pltpu.VMEM((rows, D_MODEL), _F32),
                        pltpu.VMEM((rows, D_MODEL), _BF16),
                        pltpu.VMEM((rows, D_FF), _BF16),
                        pltpu.VMEM((4, rows, LANES), _F32),
                        pltpu.VMEM((rows + CONV_SPAN, CONV_WIDTH), _F32),
                        pltpu.VMEM((SUBLANES, CONV_TILE + CONV_SPAN - SUBLANES, CONV_WIDTH), _F32),
                        pltpu.VMEM((rows, ATTN_WIDTH + CONV_WIDTH), _BF16)],
        compiler_params=pltpu.CompilerParams(dimension_semantics=("arbitrary",),
                                             vmem_limit_bytes=VMEM_LIMIT_BYTES),
    )(x1, *outs, *lses, glu, glu, glu, *weights)


def _ffn_weights(pre_g, w_gu, w_down, post_g):
    chunked = lambda w: w.reshape(D_MODEL, N_FF_CHUNKS, FF_CHUNK).transpose(1, 0, 2).astype(_BF16)
    return (pre_g.reshape(1, D_MODEL), chunked(w_gu[:, :D_FF]), chunked(w_gu[:, D_FF:]),
            w_down.astype(_BF16), post_g.reshape(1, D_MODEL))


def _layer(x, ffn1, mix_g, w_in_parts, conv_w, conv_b, ln_g, ln_b, w_out, mix_post_g, ffn2, final_g):
    batch, seq, _ = x.shape
    x1, *qkv, glu = _ffn_proj(x.reshape(batch * seq, D_MODEL), ffn1 + (mix_g,) + w_in_parts)
    per_batch = lambda a: a.reshape(batch, a.shape[0] // batch, a.shape[1])
    n = len(DILATIONS)
    branch_outs = [_attn_branch(per_batch(qkv[b]), per_batch(qkv[n + b]), per_batch(qkv[2 * n + b]), d)
                   for b, d in enumerate(DILATIONS)]
    y = _tail_ffn(per_batch(x1), branch_outs, per_batch(glu), conv_w, conv_b, ln_g, ln_b, w_out, mix_post_g,
                  ffn2, final_g)
    return y.reshape(batch, seq, D_MODEL)


def _layer_weights(l, ffn1_pre_g, ffn1_w_gu, ffn1_w_down, ffn1_post_g, mix_pre_g, w_in, conv_w, conv_b,
                   conv_ln_g, conv_ln_b, w_out, mix_post_g, ffn2_pre_g, ffn2_w_gu, ffn2_w_down, ffn2_post_g,
                   final_g):
    w_in_l = w_in[l].astype(_BF16)
    splits = (0, ATTN_WIDTH, 2 * ATTN_WIDTH, 3 * ATTN_WIDTH, 3 * ATTN_WIDTH + CONV_WIDTH,
              3 * ATTN_WIDTH + 2 * CONV_WIDTH)
    w_in_parts = tuple(w_in_l[:, a:b] for a, b in zip(splits[:-1], splits[1:]))
    row = lambda g, n: g[l].reshape(1, n)
    return (_ffn_weights(ffn1_pre_g[l], ffn1_w_gu[l], ffn1_w_down[l], ffn1_post_g[l]),
            row(mix_pre_g, D_MODEL), w_in_parts,
            conv_w[l].reshape(CONV_KERNEL, CONV_WIDTH), row(conv_b, CONV_WIDTH),
            row(conv_ln_g, CONV_WIDTH), row(conv_ln_b, CONV_WIDTH),
            w_out[l].astype(_BF16), row(mix_post_g, D_MODEL),
            _ffn_weights(ffn2_pre_g[l], ffn2_w_gu[l], ffn2_w_down[l], ffn2_post_g[l]),
            row(final_g, D_MODEL))


def kernel(x_prompt, x_sample, ffn1_pre_g, ffn1_w_gu, ffn1_w_down, ffn1_post_g, mix_pre_g, w_in, conv_w, conv_b, conv_ln_g, conv_ln_b, w_out, mix_post_g, ffn2_pre_g, ffn2_w_gu, ffn2_w_down, ffn2_post_g, final_g):
    params = (ffn1_pre_g, ffn1_w_gu, ffn1_w_down, ffn1_post_g, mix_pre_g, w_in, conv_w, conv_b, conv_ln_g,
              conv_ln_b, w_out, mix_post_g, ffn2_pre_g, ffn2_w_gu, ffn2_w_down, ffn2_post_g, final_g)
    layers = [_layer_weights(l, *params) for l in range(ffn1_pre_g.shape[0])]
    outs = []
    for x in (x_prompt, x_sample):
        for layer in layers:
            x = _layer(x, *layer)
        outs.append(x)
    return tuple(outs)
```

```python
import functools

import jax
import jax.numpy as jnp
from jax import lax
from jax.experimental import pallas as pl
from jax.experimental.pallas import tpu as pltpu

D_MODEL = 1024
D_FF = 2816
ATTN_WIDTH = 512
CONV_WIDTH = 512
HEAD_DIM = 64
N_HEADS = ATTN_WIDTH // HEAD_DIM
HEAD_PAIRS = N_HEADS // 2
DILATED_BRANCHES = ((128, 1), (512, 4), (2048, 16))
DILATIONS = tuple(d for _, d in DILATED_BRANCHES)
HALF = 64
CONV_KERNEL = 31
CONV_PAD = CONV_KERNEL // 2
FFN_RESIDUAL_WEIGHT = 0.5
RMS_EPS = 1e-6
LN_EPS = 1e-5
NEG_INF = -1e30

LANES = 128
SUBLANES = 8
LANE_CHUNKS = ATTN_WIDTH // LANES
Q_SUB = 2 * HALF
K_WIN = Q_SUB + 2 * HALF
FF_CHUNK = 256
N_FF_CHUNKS = D_FF // FF_CHUNK
ATTN_STEP_POSITIONS = 2048
FFN_TILE = 512
TAIL_TILE = 512
CONV_HALO = 16
CONV_TILE = 64
CONV_ROWS = 32
CONV_FIRST = CONV_HALO - CONV_PAD
CONV_SPAN = -(-(CONV_FIRST + CONV_KERNEL - 1) // SUBLANES) * SUBLANES
VMEM_LIMIT_BYTES = 62 * 1024 * 1024

assert all(w // (2 * d) == HALF for w, d in DILATED_BRANCHES)
assert DILATIONS[0] == 1 and D_FF % FF_CHUNK == 0

_BF16 = jnp.bfloat16
_F32 = jnp.float32


def _rms(x, g):
    ms = jnp.mean(x * x, axis=-1, keepdims=True)
    return x * lax.rsqrt(ms + RMS_EPS) * g


def _zero_token(x):
    bits = lax.bitcast_convert_type(x, jnp.uint32)
    return lax.shift_right_logical(lax.shift_right_logical(bits, jnp.uint32(16)), jnp.uint32(16))


def _whole_vmem():
    return pl.BlockSpec(memory_space=pltpu.VMEM)


def _swiglu_chunk(c, h_scr, a_scr, wg, wu):
    h = h_scr[...]
    gate = jnp.dot(h, wg[c], preferred_element_type=_F32)
    up = jnp.dot(h, wu[c], preferred_element_type=_F32)
    a_scr[:, c * FF_CHUNK:(c + 1) * FF_CHUNK] = (gate * jax.nn.sigmoid(gate) * up).astype(_BF16)


def _swiglu(h_scr, a_scr, wg, wu, wd):
    for c in range(N_FF_CHUNKS):
        _swiglu_chunk(c, h_scr, a_scr, wg, wu)
    return jnp.dot(a_scr[...], wd[...], preferred_element_type=_F32)


def _store_views(proj, view_refs, lane_scr):
    rows = proj.shape[0]
    view_refs[0][...] = proj.astype(_BF16)
    for c in range(LANE_CHUNKS):
        lane_scr[c] = proj[:, c * LANES:(c + 1) * LANES]
    for dilation, ref in zip(DILATIONS[1:], view_refs[1:]):
        for r in range(dilation):
            for c in range(LANE_CHUNKS):
                col = r * ATTN_WIDTH + c * LANES
                ref[:, col:col + LANES] = lane_scr[c, pl.ds(r, rows // dilation, stride=dilation), :].astype(_BF16)


def _ffn_proj_kernel(x_ref, pre_g, wg, wu, wd, post_g, mix_g, wq, wk, wv, wa, wb,
                     x1_ref, q1_ref, q4_ref, q16_ref, k1_ref, k4_ref, k16_ref, v1_ref, v4_ref, v16_ref, glu_ref,
                     h_scr, a_scr, lane_scr):
    h_scr[...] = _rms(x_ref[...], pre_g[...]).astype(_BF16)
    x1 = x_ref[...] + FFN_RESIDUAL_WEIGHT * _rms(_swiglu(h_scr, a_scr, wg, wu, wd), post_g[...])
    x1_ref[...] = x1
    h_scr[...] = _rms(x1, mix_g[...]).astype(_BF16)
    q = jnp.dot(h_scr[...], wq[...], preferred_element_type=_F32) * (HEAD_DIM ** -0.5)
    _store_views(q, (q1_ref, q4_ref, q16_ref), lane_scr)
    k = jnp.dot(h_scr[...], wk[...], preferred_element_type=_F32)
    _store_views(k, (k1_ref, k4_ref, k16_ref), lane_scr)
    v = jnp.dot(h_scr[...], wv[...], preferred_element_type=_F32)
    _store_views(v, (v1_ref, v4_ref, v16_ref), lane_scr)
    ga = jnp.dot(h_scr[...], wa[...], preferred_element_type=_F32)
    gb = jnp.dot(h_scr[...], wb[...], preferred_element_type=_F32)
    glu_ref[...] = ga * jax.nn.sigmoid(gb)


def _ffn_proj(x, weights):
    n_tokens = x.shape[0]
    tm = FFN_TILE
    row_spec = lambda d, w: pl.BlockSpec((tm // d, d * w), lambda i: (i, 0))
    outs = ([(1, D_MODEL, _F32)] + [(d, ATTN_WIDTH, _BF16) for _ in range(3) for d in DILATIONS]
            + [(1, CONV_WIDTH, _F32)])
    return pl.pallas_call(
        _ffn_proj_kernel,
        name="ffn_proj",
        grid=(n_tokens // tm,),
        in_specs=[row_spec(1, D_MODEL)] + [_whole_vmem() for _ in weights],
        out_specs=[row_spec(d, w) for d, w, _ in outs],
        out_shape=[jax.ShapeDtypeStruct((n_tokens // d, d * w), dt) for d, w, dt in outs],
        scratch_shapes=[pltpu.VMEM((tm, D_MODEL), _BF16), pltpu.VMEM((tm, D_FF), _BF16),
                        pltpu.VMEM((LANE_CHUNKS, tm, LANES), _F32)],
        compiler_params=pltpu.CompilerParams(dimension_semantics=("arbitrary",),
                                             vmem_limit_bytes=VMEM_LIMIT_BYTES),
    )(x, *weights)


def _attn_kernel(q_ref, kp_ref, k_ref, kn_ref, vp_ref, v_ref, vn_ref, o_ref, lse_ref, bias_scr,
                 *, dilation, rows, residues):
    j = pl.program_id(2)
    n_sub = rows // Q_SUB

    @pl.when((pl.program_id(0) == 0) & (pl.program_id(1) == 0) & (j == 0))
    def _():
        row = lax.broadcasted_iota(jnp.int32, (2 * Q_SUB, K_WIN), 0)
        col = lax.broadcasted_iota(jnp.int32, (2 * Q_SUB, K_WIN), 1)
        rel = jnp.abs(col - HALF - (row & (Q_SUB - 1)))
        dist = rel.astype(_F32)
        for p in range(HEAD_PAIRS):
            slope_even = dilation * 2.0 ** (-8.0 * (2 * p + 1) / N_HEADS)
            slope_odd = dilation * 2.0 ** (-8.0 * (2 * p + 2) / N_HEADS)
            slope = jnp.where(row < Q_SUB, slope_even, slope_odd)
            bias_scr[p] = jnp.where(rel <= HALF, -slope * dist, NEG_INF)

    lane = lax.broadcasted_iota(jnp.int32, (Q_SUB, LANES), 1)
    even_lanes = lane < HEAD_DIM
    col = lax.broadcasted_iota(jnp.int32, (2 * Q_SUB, K_WIN), 1)
    first_valid_col = jnp.where(j == 0, HALF, 0)
    last_valid_col = jnp.where(j == pl.num_programs(2) - 1, K_WIN - HALF, K_WIN)
    ones = jnp.ones((K_WIN, LANES), _BF16)

    def window(prev_ref, main_ref, next_ref, i, lanes):
        parts = []
        lo = i * Q_SUB - HALF
        hi = lo + K_WIN
        if lo < 0:
            parts.append(prev_ref[0, :, lanes])
            lo = 0
        parts.append(main_ref[0, lo:min(hi, rows), lanes])
        if hi > rows:
            parts.append(next_ref[0, :, lanes])
        return parts[0] if len(parts) == 1 else jnp.concatenate(parts, axis=0)

    for res, i, p in ((res, i, p) for res in range(residues) for i in range(n_sub) for p in range(HEAD_PAIRS)):
        q_rows = slice(i * Q_SUB, (i + 1) * Q_SUB)
        lanes = slice(res * ATTN_WIDTH + p * LANES, res * ATTN_WIDTH + (p + 1) * LANES)
        q2 = q_ref[0, q_rows, lanes]
        zero = jnp.zeros_like(q2)
        q_stack = jnp.concatenate([jnp.where(even_lanes, q2, zero), jnp.where(even_lanes, zero, q2)], axis=0)
        kw = window(kp_ref, k_ref, kn_ref, i, lanes)
        s = lax.dot_general(q_stack, kw, (((1,), (1,)), ((), ())), preferred_element_type=_F32)
        s = s + bias_scr[p]
        if i == 0:
            s = jnp.where(col < first_valid_col, NEG_INF, s)
        if i == n_sub - 1:
            s = jnp.where(col >= last_valid_col, NEG_INF, s)
        m = jnp.max(s, axis=-1, keepdims=True)
        e = jnp.exp(s - m).astype(_BF16)
        vw = window(vp_ref, v_ref, vn_ref, i, lanes)
        r = jnp.dot(e, jnp.concatenate([vw, ones], axis=1), preferred_element_type=_F32)
        acc = jnp.where(even_lanes, r[:Q_SUB, :LANES], r[Q_SUB:, :LANES])
        den = jnp.where(even_lanes, r[:Q_SUB, LANES:], r[Q_SUB:, LANES:])
        mx = jnp.where(even_lanes, m[:Q_SUB], m[Q_SUB:])
        o_ref[0, q_rows, lanes] = acc / den
        lse_ref[0, q_rows, lanes] = mx + jnp.log(den)


def _attn_branch(q, k, v, dilation):
    batch, sub_len, _ = q.shape
    rows = min(sub_len, ATTN_STEP_POSITIONS)
    residues = min(dilation, ATTN_STEP_POSITIONS // rows)
    width = residues * ATTN_WIDTH
    halo_per_tile = rows // HALF
    last_halo = sub_len // HALF - 1
    main = pl.BlockSpec((1, rows, width), lambda b, r, j: (b, j, r))
    prev = pl.BlockSpec((1, HALF, width), lambda b, r, j: (b, jnp.maximum(j * halo_per_tile - 1, 0), r))
    nxt = pl.BlockSpec((1, HALF, width), lambda b, r, j: (b, jnp.minimum((j + 1) * halo_per_tile, last_halo), r))
    out_shape = jax.ShapeDtypeStruct((batch, sub_len, dilation * ATTN_WIDTH), _F32)
    return pl.pallas_call(
        functools.partial(_attn_kernel, dilation=dilation, rows=rows, residues=residues),
        name=f"attn_dilation{dilation}",
        grid=(batch, dilation // residues, sub_len // rows),
        in_specs=[main, prev, main, nxt, prev, main, nxt],
        out_specs=[main, main],
        out_shape=[out_shape, out_shape],
        scratch_shapes=[pltpu.VMEM((HEAD_PAIRS, 2 * Q_SUB, K_WIN), _F32)],
        compiler_params=pltpu.CompilerParams(dimension_semantics=("arbitrary",) * 3,
                                             vmem_limit_bytes=VMEM_LIMIT_BYTES),
    )(q, k, k, k, v, v, v)


def _tail_ffn_kernel(x1_ref, o1, o4, o16, l1, l4, l16, gp_ref, g_ref, gn_ref,
                     conv_w, conv_b, ln_g, ln_b, w_out, post_g,
                     pre_g, wg, wu, wd, ffn_post_g, final_g,
                     y_ref,
                     x2_scr, h_scr, a_scr, nat_scr, g_scr, shift_scr, m_scr,
                     *, rows, tiles_per_seq, n_tiles):
    s = pl.program_id(0)
    i = jnp.minimum(s, n_tiles - 1) % tiles_per_seq

    @pl.when(s == 0)
    def _():
        x2_scr[...] = jnp.zeros_like(x2_scr)

    h_scr[...] = _rms(x2_scr[...], pre_g[...]).astype(_BF16)

    def merge_item(c):
        def run():
            lanes = slice(c * LANES, (c + 1) * LANES)
            for a, ref in enumerate((o4, l4, o16, l16)):
                dilation = DILATIONS[1 + a // 2]
                for r in range(dilation):
                    col = r * ATTN_WIDTH + c * LANES
                    nat_scr[a, pl.ds(r, rows // dilation, stride=dilation), :] = ref[0, :, col:col + LANES]
            outs = (o1[0, :, lanes], nat_scr[0], nat_scr[2])
            lses = (l1[0, :, lanes], nat_scr[1], nat_scr[3])
            top = jnp.maximum(jnp.maximum(lses[0], lses[1]), lses[2])
            w = [jnp.exp(l - top) for l in lses]
            attn = (w[0] * outs[0] + w[1] * outs[1] + w[2] * outs[2]) / (w[0] + w[1] + w[2])
            m_scr[:, lanes] = attn.astype(_BF16)
            return attn[-SUBLANES:, :]
        return run

    def conv_input_item():
        last = jnp.where(i < tiles_per_seq - 1, gn_ref[0], 0.0)
        g_scr[:CONV_HALO] = jnp.where(i > 0, gp_ref[0], 0.0)
        g_scr[CONV_HALO:CONV_HALO + rows] = g_ref[0]
        g_scr[CONV_HALO + rows:CONV_HALO + rows + CONV_HALO] = last
        return last[-SUBLANES:, :LANES]

    def shift_item(t0):
        def run():
            for b in range(SUBLANES):
                shifted = g_scr[t0 + b:t0 + b + CONV_TILE + CONV_SPAN - SUBLANES, :]
                shift_scr[b] = shifted
            return shifted[-SUBLANES:, :LANES]
        return run

    def conv_item(t0, base):
        def run():
            acc = jnp.broadcast_to(conv_b[...], (CONV_ROWS, CONV_WIDTH))
            for t in range(CONV_KERNEL):
                off = CONV_FIRST + t
                start = base + off // SUBLANES * SUBLANES
                acc = acc + conv_w[t:t + 1, :] * shift_scr[off % SUBLANES, start:start + CONV_ROWS, :]
            mu = jnp.mean(acc, axis=-1, keepdims=True)
            cen = acc - mu
            var = jnp.mean(cen * cen, axis=-1, keepdims=True)
            y = cen * lax.rsqrt(var + LN_EPS) * ln_g[...] + ln_b[...]
            act = y * jax.nn.sigmoid(y)
            m_scr[t0 + base:t0 + base + CONV_ROWS, ATTN_WIDTH:] = act.astype(_BF16)
            return act[-SUBLANES:, :LANES]
        return run

    tail_items = [merge_item(c) for c in range(LANE_CHUNKS)] + [conv_input_item]
    for t0 in range(0, rows, CONV_TILE):
        tail_items.append(shift_item(t0))
        tail_items += [conv_item(t0, base) for base in range(0, CONV_TILE, CONV_ROWS)]

    def anchor(ref, token):
        bits = pltpu.bitcast(ref[:2 * SUBLANES, :LANES], jnp.uint32)
        ref[:2 * SUBLANES, :LANES] = pltpu.bitcast(bits | token, _BF16)

    done = 0
    for c in range(N_FF_CHUNKS):
        _swiglu_chunk(c, h_scr, a_scr, wg, wu)
        upto = len(tail_items) * (c + 1) // N_FF_CHUNKS
        token = None
        for item in tail_items[done:upto]:
            t = _zero_token(item())
            token = t if token is None else token | t
        done = upto
        if token is not None and c + 1 < N_FF_CHUNKS:
            anchor(h_scr, token)

    mixed = jnp.dot(m_scr[...], w_out[...], preferred_element_type=_F32)
    ffn = jnp.dot(a_scr[...], wd[...], preferred_element_type=_F32)

    x3 = x2_scr[...] + FFN_RESIDUAL_WEIGHT * _rms(ffn, ffn_post_g[...])
    y_ref[...] = _rms(x3, final_g[...])
    x2_scr[...] = x1_ref[0] + _rms(mixed, post_g[...])


def _tail_ffn(x1, branch_outs, glu, conv_w, conv_b, ln_g, ln_b, w_out, post_g, ffn2, final_g):
    batch, seq, _ = x1.shape
    rows = TAIL_TILE
    tiles_per_seq = seq // rows
    n_tiles = batch * tiles_per_seq
    halo_per_tile = rows // CONV_HALO
    last_halo = seq // CONV_HALO - 1

    def tail_tile(s):
        t = jnp.minimum(s, n_tiles - 1)
        return t // tiles_per_seq, t % tiles_per_seq

    def tile(w, d=1):
        return pl.BlockSpec((1, rows // d, d * w), lambda s: (*tail_tile(s), 0))

    def halo(offset):
        def index(s):
            b, i = tail_tile(s)
            return b, jnp.clip(i * halo_per_tile + offset, 0, last_halo), 0
        return pl.BlockSpec((1, CONV_HALO, CONV_WIDTH), index)

    outs = [o for o, _ in branch_outs]
    lses = [l for _, l in branch_outs]
    views = [tile(ATTN_WIDTH, d) for d in DILATIONS]
    weights = (conv_w, conv_b, ln_g, ln_b, w_out, post_g) + ffn2 + (final_g,)
    return pl.pallas_call(
        functools.partial(_tail_ffn_kernel, rows=rows, tiles_per_seq=tiles_per_seq, n_tiles=n_tiles),
        name="tail_ffn",
        grid=(n_tiles + 1,),
        in_specs=[tile(D_MODEL)] + views + views + [halo(-1), tile(CONV_WIDTH), halo(halo_per_tile)]
                 + [_whole_vmem() for _ in weights],
        out_specs=pl.BlockSpec((rows, D_MODEL), lambda s: (jnp.maximum(s - 1, 0), 0)),
        out_shape=jax.ShapeDtypeStruct((batch * seq, D_MODEL), _F32),
        scratch_shapes=[pltpu.VMEM((rows, D_MODEL), _F32),
                        pltpu.VMEM((rows, D_MODEL), _BF16),
                        pltpu.VMEM((rows, D_FF), _BF16),
                        pltpu.VMEM((4, rows, LANES), _F32),
                        pltpu.VMEM((rows + CONV_SPAN, CONV_WIDTH), _F32),
                        pltpu.VMEM((SUBLANES, CONV_TILE + CONV_SPAN - SUBLANES, CONV_WIDTH), _F32),
                        pltpu.VMEM((rows, ATTN_WIDTH + CONV_WIDTH), _BF16)],
        compiler_params=pltpu.CompilerParams(dimension_semantics=("arbitrary",),
                                             vmem_limit_bytes=VMEM_LIMIT_BYTES),
    )(x1, *outs, *lses, glu, glu, glu, *weights)


def _ffn_weights(pre_g, w_gu, w_down, post_g):
    chunked = lambda w: w.reshape(D_MODEL, N_FF_CHUNKS, FF_CHUNK).transpose(1, 0, 2).astype(_BF16)
    return (pre_g.reshape(1, D_MODEL), chunked(w_gu[:, :D_FF]), chunked(w_gu[:, D_FF:]),
            w_down.astype(_BF16), post_g.reshape(1, D_MODEL))


def _layer(x, ffn1, mix_g, w_in_parts, conv_w, conv_b, ln_g, ln_b, w_out, mix_post_g, ffn2, final_g):
    batch, seq, _ = x.shape
    x1, *qkv, glu = _ffn_proj(x.reshape(batch * seq, D_MODEL), ffn1 + (mix_g,) + w_in_parts)
    per_batch = lambda a: a.reshape(batch, a.shape[0] // batch, a.shape[1])
    n = len(DILATIONS)
    branch_outs = [_attn_branch(per_batch(qkv[b]), per_batch(qkv[n + b]), per_batch(qkv[2 * n + b]), d)
                   for b, d in enumerate(DILATIONS)]
    y = _tail_ffn(per_batch(x1), branch_outs, per_batch(glu), conv_w, conv_b, ln_g, ln_b, w_out, mix_post_g,
                  ffn2, final_g)
    return y.reshape(batch, seq, D_MODEL)


def _layer_weights(l, ffn1_pre_g, ffn1_w_gu, ffn1_w_down, ffn1_post_g, mix_pre_g, w_in, conv_w, conv_b,
                   conv_ln_g, conv_ln_b, w_out, mix_post_g, ffn2_pre_g, ffn2_w_gu, ffn2_w_down, ffn2_post_g,
                   final_g):
    w_in_l = w_in[l].astype(_BF16)
    splits = (0, ATTN_WIDTH, 2 * ATTN_WIDTH, 3 * ATTN_WIDTH, 3 * ATTN_WIDTH + CONV_WIDTH,
              3 * ATTN_WIDTH + 2 * CONV_WIDTH)
    w_in_parts = tuple(w_in_l[:, a:b] for a, b in zip(splits[:-1], splits[1:]))
    row = lambda g, n: g[l].reshape(1, n)
    return (_ffn_weights(ffn1_pre_g[l], ffn1_w_gu[l], ffn1_w_down[l], ffn1_post_g[l]),
            row(mix_pre_g, D_MODEL), w_in_parts,
            conv_w[l].reshape(CONV_KERNEL, CONV_WIDTH), row(conv_b, CONV_WIDTH),
            row(conv_ln_g, CONV_WIDTH), row(conv_ln_b, CONV_WIDTH),
            w_out[l].astype(_BF16), row(mix_post_g, D_MODEL),
            _ffn_weights(ffn2_pre_g[l], ffn2_w_gu[l], ffn2_w_down[l], ffn2_post_g[l]),
            row(final_g, D_MODEL))


def kernel(x_prompt, x_sample, ffn1_pre_g, ffn1_w_gu, ffn1_w_down, ffn1_post_g, mix_pre_g, w_in, conv_w, conv_b, conv_ln_g, conv_ln_b, w_out, mix_post_g, ffn2_pre_g, ffn2_w_gu, ffn2_w_down, ffn2_post_g, final_g):
    params = (ffn1_pre_g, ffn1_w_gu, ffn1_w_down, ffn1_post_g, mix_pre_g, w_in, conv_w, conv_b, conv_ln_g,
              conv_ln_b, w_out, mix_post_g, ffn2_pre_g, ffn2_w_gu, ffn2_w_down, ffn2_post_g, final_g)
    layers = [_layer_weights(l, *params) for l in range(ffn1_pre_g.shape[0])]
    outs = []
    for x in (x_prompt, x_sample):
        for layer in layers:
            x = _layer(x, *layer)
        outs.append(x)
    return tuple(outs)
```

```python
import functools
import math

import jax
import jax.numpy as jnp
from jax import lax
from jax.experimental import pallas as pl
from jax.experimental.pallas import tpu as pltpu

D_MODEL = 1024
D_FF = 2816
ATTN_WIDTH = 512
CONV_WIDTH = 512
HEAD_DIM = 64
N_HEADS = ATTN_WIDTH // HEAD_DIM
HEAD_PAIRS = N_HEADS // 2
DILATED_BRANCHES = ((128, 1), (512, 4), (2048, 16))
DILATIONS = tuple(d for _, d in DILATED_BRANCHES)
HALF = 64
CONV_KERNEL = 31
CONV_PAD = CONV_KERNEL // 2
FFN_RESIDUAL_WEIGHT = 0.5
RMS_EPS = 1e-6
LN_EPS = 1e-5
NEG_INF = -1e30

LANES = 128
SUBLANES = 8
LANE_CHUNKS = ATTN_WIDTH // LANES
Q_SUB = 2 * HALF
K_WIN = Q_SUB + 2 * HALF
FF_CHUNK = 256
N_FF_CHUNKS = D_FF // FF_CHUNK
ATTN_STEP_POSITIONS = 2048
FFN_TILE = 512
TAIL_TILE = 512
CONV_HALO = 16
CONV_TILE = 64
CONV_ROWS = 32
CONV_FIRST = CONV_HALO - CONV_PAD
CONV_SPAN = -(-(CONV_FIRST + CONV_KERNEL - 1) // SUBLANES) * SUBLANES
VMEM_LIMIT_BYTES = 56 * 1024 * 1024

assert all(w // (2 * d) == HALF for w, d in DILATED_BRANCHES)
assert DILATIONS[0] == 1 and D_FF % FF_CHUNK == 0
assert math.frexp(FFN_RESIDUAL_WEIGHT)[0] == 0.5

_BF16 = jnp.bfloat16
_F32 = jnp.float32


def _rms(x, g):
    ms = jnp.mean(x * x, axis=-1, keepdims=True)
    return x * lax.rsqrt(ms + RMS_EPS) * g


def _whole_vmem():
    return pl.BlockSpec(memory_space=pltpu.VMEM)


def _swiglu(h_scr, a_scr, wgu, wd):
    for c in range(N_FF_CHUNKS):
        h = h_scr[...]
        gate = jnp.dot(h, wgu[:, c * FF_CHUNK:(c + 1) * FF_CHUNK], preferred_element_type=_F32)
        up = jnp.dot(h, wgu[:, D_FF + c * FF_CHUNK:D_FF + (c + 1) * FF_CHUNK], preferred_element_type=_F32)
        a_scr[:, c * FF_CHUNK:(c + 1) * FF_CHUNK] = (gate * jax.nn.sigmoid(gate) * up).astype(_BF16)
    return jnp.dot(a_scr[...], wd[...], preferred_element_type=_F32)


def _store_views(proj, view_refs, lane_scr):
    rows = proj.shape[0]
    view_refs[0][...] = proj.astype(_BF16)
    for c in range(LANE_CHUNKS):
        lane_scr[c] = proj[:, c * LANES:(c + 1) * LANES]
    for dilation, ref in zip(DILATIONS[1:], view_refs[1:]):
        for r in range(dilation):
            for c in range(LANE_CHUNKS):
                col = r * ATTN_WIDTH + c * LANES
                ref[:, col:col + LANES] = lane_scr[c, pl.ds(r, rows // dilation, stride=dilation), :].astype(_BF16)


def _ffn_proj_kernel(x_ref, pre_g, wgu, wd, post_g, mix_g, wq, wk, wv, wa, wb,
                     x1_ref, q1_ref, q4_ref, q16_ref, k1_ref, k4_ref, k16_ref, v1_ref, v4_ref, v16_ref, glu_ref,
                     h_scr, a_scr, lane_scr):
    h_scr[...] = _rms(x_ref[...], pre_g[...]).astype(_BF16)
    x1 = x_ref[...] + _rms(_swiglu(h_scr, a_scr, wgu, wd), post_g[...])
    x1_ref[...] = x1
    h_scr[...] = _rms(x1, mix_g[...]).astype(_BF16)
    q = jnp.dot(h_scr[...], wq[...], preferred_element_type=_F32) * (HEAD_DIM ** -0.5)
    _store_views(q, (q1_ref, q4_ref, q16_ref), lane_scr)
    k = jnp.dot(h_scr[...], wk[...], preferred_element_type=_F32)
    _store_views(k, (k1_ref, k4_ref, k16_ref), lane_scr)
    v = jnp.dot(h_scr[...], wv[...], preferred_element_type=_F32)
    _store_views(v, (v1_ref, v4_ref, v16_ref), lane_scr)
    ga = jnp.dot(h_scr[...], wa[...], preferred_element_type=_F32)
    gb = jnp.dot(h_scr[...], wb[...], preferred_element_type=_F32)
    glu_ref[...] = ga * jax.nn.sigmoid(gb)


def _ffn_final_kernel(x_ref, pre_g, wgu, wd, post_g, final_g, y_ref, h_scr, a_scr):
    h_scr[...] = _rms(x_ref[...], pre_g[...]).astype(_BF16)
    x3 = x_ref[...] + _rms(_swiglu(h_scr, a_scr, wgu, wd), post_g[...])
    y_ref[...] = _rms(x3, final_g[...])


def _ffn_call(body, name, x, weights, outs, extra_scratch=()):
    n_tokens = x.shape[0]
    tm = FFN_TILE
    row_spec = lambda d, w: pl.BlockSpec((tm // d, d * w), lambda i: (i, 0))
    return pl.pallas_call(
        body,
        name=name,
        grid=(n_tokens // tm,),
        in_specs=[row_spec(1, D_MODEL)] + [_whole_vmem() for _ in weights],
        out_specs=[row_spec(d, w) for d, w, _ in outs],
        out_shape=[jax.ShapeDtypeStruct((n_tokens // d, d * w), dt) for d, w, dt in outs],
        scratch_shapes=[pltpu.VMEM((tm, D_MODEL), _BF16), pltpu.VMEM((tm, D_FF), _BF16), *extra_scratch],
        compiler_params=pltpu.CompilerParams(dimension_semantics=("arbitrary",),
                                             vmem_limit_bytes=VMEM_LIMIT_BYTES),
    )(x, *weights)


def _attn_kernel(q_ref, kp_ref, k_ref, kn_ref, vp_ref, v_ref, vn_ref, o_ref, lse_ref, bias_scr,
                 *, dilation, rows, residues):
    j = pl.program_id(2)
    n_sub = rows // Q_SUB

    @pl.when((pl.program_id(0) == 0) & (pl.program_id(1) == 0) & (j == 0))
    def _():
        row = lax.broadcasted_iota(jnp.int32, (2 * Q_SUB, K_WIN), 0)
        col = lax.broadcasted_iota(jnp.int32, (2 * Q_SUB, K_WIN), 1)
        rel = jnp.abs(col - HALF - (row & (Q_SUB - 1)))
        dist = rel.astype(_F32)
        for p in range(HEAD_PAIRS):
            slope_even = dilation * 2.0 ** (-8.0 * (2 * p + 1) / N_HEADS)
            slope_odd = dilation * 2.0 ** (-8.0 * (2 * p + 2) / N_HEADS)
            slope = jnp.where(row < Q_SUB, slope_even, slope_odd)
            bias_scr[p] = jnp.where(rel <= HALF, -slope * dist, NEG_INF)

    lane = lax.broadcasted_iota(jnp.int32, (Q_SUB, LANES), 1)
    even_lanes = lane < HEAD_DIM
    col = lax.broadcasted_iota(jnp.int32, (2 * Q_SUB, K_WIN), 1)
    first_valid_col = jnp.where(j == 0, HALF, 0)
    last_valid_col = jnp.where(j == pl.num_programs(2) - 1, K_WIN - HALF, K_WIN)
    ones = jnp.ones((K_WIN, LANES), _BF16)

    def window(prev_ref, main_ref, next_ref, i, lanes):
        parts = []
        lo = i * Q_SUB - HALF
        hi = lo + K_WIN
        if lo < 0:
            parts.append(prev_ref[0, :, lanes])
            lo = 0
        parts.append(main_ref[0, lo:min(hi, rows), lanes])
        if hi > rows:
            parts.append(next_ref[0, :, lanes])
        return parts[0] if len(parts) == 1 else jnp.concatenate(parts, axis=0)

    for res, i, p in ((res, i, p) for res in range(residues) for i in range(n_sub) for p in range(HEAD_PAIRS)):
        q_rows = slice(i * Q_SUB, (i + 1) * Q_SUB)
        lanes = slice(res * ATTN_WIDTH + p * LANES, res * ATTN_WIDTH + (p + 1) * LANES)
        q2 = q_ref[0, q_rows, lanes]
        zero = jnp.zeros_like(q2)
        q_stack = jnp.concatenate([jnp.where(even_lanes, q2, zero), jnp.where(even_lanes, zero, q2)], axis=0)
        kw = window(kp_ref, k_ref, kn_ref, i, lanes)
        s = lax.dot_general(q_stack, kw, (((1,), (1,)), ((), ())), preferred_element_type=_F32)
        s = s + bias_scr[p]
        if i == 0:
            s = jnp.where(col < first_valid_col, NEG_INF, s)
        if i == n_sub - 1:
            s = jnp.where(col >= last_valid_col, NEG_INF, s)
        m = jnp.max(s, axis=-1, keepdims=True)
        e = jnp.exp(s - m).astype(_BF16)
        vw = window(vp_ref, v_ref, vn_ref, i, lanes)
        r = jnp.dot(e, jnp.concatenate([vw, ones], axis=1), preferred_element_type=_F32)
        acc = jnp.where(even_lanes, r[:Q_SUB, :LANES], r[Q_SUB:, :LANES])
        den = jnp.where(even_lanes, r[:Q_SUB, LANES:], r[Q_SUB:, LANES:])
        mx = jnp.where(even_lanes, m[:Q_SUB], m[Q_SUB:])
        o_ref[0, q_rows, lanes] = acc / den
        lse_ref[0, q_rows, lanes] = mx + jnp.log(den)


def _attn_branch(q, k, v, dilation):
    batch, sub_len, _ = q.shape
    rows = min(sub_len, ATTN_STEP_POSITIONS)
    residues = min(dilation, ATTN_STEP_POSITIONS // rows)
    width = residues * ATTN_WIDTH
    halo_per_tile = rows // HALF
    last_halo = sub_len // HALF - 1
    main = pl.BlockSpec((1, rows, width), lambda b, r, j: (b, j, r))
    prev = pl.BlockSpec((1, HALF, width), lambda b, r, j: (b, jnp.maximum(j * halo_per_tile - 1, 0), r))
    nxt = pl.BlockSpec((1, HALF, width), lambda b, r, j: (b, jnp.minimum((j + 1) * halo_per_tile, last_halo), r))
    out_shape = jax.ShapeDtypeStruct((batch, sub_len, dilation * ATTN_WIDTH), _F32)
    return pl.pallas_call(
        functools.partial(_attn_kernel, dilation=dilation, rows=rows, residues=residues),
        name=f"attn_dilation{dilation}",
        grid=(batch, dilation // residues, sub_len // rows),
        in_specs=[main, prev, main, nxt, prev, main, nxt],
        out_specs=[main, main],
        out_shape=[out_shape, out_shape],
        scratch_shapes=[pltpu.VMEM((HEAD_PAIRS, 2 * Q_SUB, K_WIN), _F32)],
        compiler_params=pltpu.CompilerParams(dimension_semantics=("arbitrary",) * 3,
                                             vmem_limit_bytes=VMEM_LIMIT_BYTES),
    )(q, k, k, k, v, v, v)


def _mixer_tail_kernel(x1_ref, o1, o4, o16, l1, l4, l16, gp_ref, g_ref, gn_ref,
                       conv_w, conv_b, ln_g, ln_b, w_out, post_g, x2_ref,
                       nat_scr, g_scr, shift_scr, m_scr, *, rows):
    i = pl.program_id(1)

    for c in range(LANE_CHUNKS):
        lanes = slice(c * LANES, (c + 1) * LANES)
        for a, ref in enumerate((o4, l4, o16, l16)):
            dilation = DILATIONS[1 + a // 2]
            for r in range(dilation):
                col = r * ATTN_WIDTH + c * LANES
                nat_scr[a, pl.ds(r, rows // dilation, stride=dilation), :] = ref[0, :, col:col + LANES]
        outs = (o1[0, :, lanes], nat_scr[0], nat_scr[2])
        lses = (l1[0, :, lanes], nat_scr[1], nat_scr[3])
        top = jnp.maximum(jnp.maximum(lses[0], lses[1]), lses[2])
        w = [jnp.exp(l - top) for l in lses]
        attn = (w[0] * outs[0] + w[1] * outs[1] + w[2] * outs[2]) / (w[0] + w[1] + w[2])
        m_scr[:, lanes] = attn.astype(_BF16)

    g_scr[:CONV_HALO] = jnp.where(i > 0, gp_ref[0], 0.0)
    g_scr[CONV_HALO:CONV_HALO + rows] = g_ref[0]
    g_scr[CONV_HALO + rows:CONV_HALO + rows + CONV_HALO] = jnp.where(i < pl.num_programs(1) - 1, gn_ref[0], 0.0)
    for t0 in range(0, rows, CONV_TILE):
        for b in range(SUBLANES):
            shift_scr[b] = g_scr[t0 + b:t0 + b + CONV_TILE + CONV_SPAN - SUBLANES, :]
        for base in range(0, CONV_TILE, CONV_ROWS):
            acc = jnp.broadcast_to(conv_b[...], (CONV_ROWS, CONV_WIDTH))
            for t in range(CONV_KERNEL):
                off = CONV_FIRST + t
                start = base + off // SUBLANES * SUBLANES
                acc = acc + conv_w[t:t + 1, :] * shift_scr[off % SUBLANES, start:start + CONV_ROWS, :]
            mu = jnp.mean(acc, axis=-1, keepdims=True)
            cen = acc - mu
            var = jnp.mean(cen * cen, axis=-1, keepdims=True)
            y = cen * lax.rsqrt(var + LN_EPS) * ln_g[...] + ln_b[...]
            m_scr[t0 + base:t0 + base + CONV_ROWS, ATTN_WIDTH:] = (y * jax.nn.sigmoid(y)).astype(_BF16)

    mixed = jnp.dot(m_scr[...], w_out[...], preferred_element_type=_F32)
    x2_ref[0] = x1_ref[0] + _rms(mixed, post_g[...])


def _mixer_tail(x1, branch_outs, glu, conv_w, conv_b, ln_g, ln_b, w_out, post_g):
    batch, seq, _ = x1.shape
    rows = TAIL_TILE
    halo_per_tile = rows // CONV_HALO
    last_halo = seq // CONV_HALO - 1
    tile = lambda w, d=1: pl.BlockSpec((1, rows // d, d * w), lambda b, i: (b, i, 0))
    prev = pl.BlockSpec((1, CONV_HALO, CONV_WIDTH), lambda b, i: (b, jnp.maximum(i * halo_per_tile - 1, 0), 0))
    nxt = pl.BlockSpec((1, CONV_HALO, CONV_WIDTH),
                       lambda b, i: (b, jnp.minimum((i + 1) * halo_per_tile, last_halo), 0))
    outs = [o for o, _ in branch_outs]
    lses = [l for _, l in branch_outs]
    views = [tile(ATTN_WIDTH, d) for d in DILATIONS]
    weights = (conv_w, conv_b, ln_g, ln_b, w_out, post_g)
    return pl.pallas_call(
        functools.partial(_mixer_tail_kernel, rows=rows),
        name="mixer_tail",
        grid=(batch, seq // rows),
        in_specs=[tile(D_MODEL)] + views + views + [prev, tile(CONV_WIDTH), nxt]
                 + [_whole_vmem() for _ in weights],
        out_specs=tile(D_MODEL),
        out_shape=jax.ShapeDtypeStruct((batch, seq, D_MODEL), _F32),
        scratch_shapes=[pltpu.VMEM((4, rows, LANES), _F32),
                        pltpu.VMEM((rows + CONV_SPAN, CONV_WIDTH), _F32),
                        pltpu.VMEM((SUBLANES, CONV_TILE + CONV_SPAN - SUBLANES, CONV_WIDTH), _F32),
                        pltpu.VMEM((rows, ATTN_WIDTH + CONV_WIDTH), _BF16)],
        compiler_params=pltpu.CompilerParams(dimension_semantics=("arbitrary",) * 2,
                                             vmem_limit_bytes=VMEM_LIMIT_BYTES),
    )(x1, *outs, *lses, glu, glu, glu, *weights)


def _ffn_weights(pre_g, w_gu, w_down, post_g):
    post_g = post_g * FFN_RESIDUAL_WEIGHT
    return (pre_g.reshape(1, D_MODEL), w_gu.astype(_BF16), w_down.astype(_BF16), post_g.reshape(1, D_MODEL))


def _layer(x, ffn1, mix_g, w_in_parts, conv_w, conv_b, ln_g, ln_b, w_out, mix_post_g, ffn2, final_g):
    batch, seq, _ = x.shape
    tokens = batch * seq
    qkv_views = [(d, ATTN_WIDTH, _BF16) for _ in range(3) for d in DILATIONS]
    x1, *qkv, glu = _ffn_call(
        _ffn_proj_kernel, "ffn_proj", x.reshape(tokens, D_MODEL), ffn1 + (mix_g,) + w_in_parts,
        [(1, D_MODEL, _F32)] + qkv_views + [(1, CONV_WIDTH, _F32)],
        extra_scratch=[pltpu.VMEM((LANE_CHUNKS, FFN_TILE, LANES), _F32)])
    per_batch = lambda a: a.reshape(batch, a.shape[0] // batch, a.shape[1])
    n = len(DILATIONS)
    branch_outs = [_attn_branch(per_batch(qkv[b]), per_batch(qkv[n + b]), per_batch(qkv[2 * n + b]), d)
                   for b, d in enumerate(DILATIONS)]
    x2 = _mixer_tail(per_batch(x1), branch_outs, per_batch(glu), conv_w, conv_b, ln_g, ln_b, w_out, mix_post_g)
    (y,) = _ffn_call(_ffn_final_kernel, "ffn_final", x2.reshape(tokens, D_MODEL), ffn2 + (final_g,),
                     [(1, D_MODEL, _F32)])
    return y.reshape(batch, seq, D_MODEL)


def _layer_weights(l, ffn1_pre_g, ffn1_w_gu, ffn1_w_down, ffn1_post_g, mix_pre_g, w_in, conv_w, conv_b,
                   conv_ln_g, conv_ln_b, w_out, mix_post_g, ffn2_pre_g, ffn2_w_gu, ffn2_w_down, ffn2_post_g,
                   final_g):
    w_in_l = w_in[l].astype(_BF16)
    splits = (0, ATTN_WIDTH, 2 * ATTN_WIDTH, 3 * ATTN_WIDTH, 3 * ATTN_WIDTH + CONV_WIDTH,
              3 * ATTN_WIDTH + 2 * CONV_WIDTH)
    w_in_parts = tuple(w_in_l[:, a:b] for a, b in zip(splits[:-1], splits[1:]))
    row = lambda g, n: g[l].reshape(1, n)
    return (_ffn_weights(ffn1_pre_g[l], ffn1_w_gu[l], ffn1_w_down[l], ffn1_post_g[l]),
            row(mix_pre_g, D_MODEL), w_in_parts,
            conv_w[l].reshape(CONV_KERNEL, CONV_WIDTH), row(conv_b, CONV_WIDTH),
            row(conv_ln_g, CONV_WIDTH), row(conv_ln_b, CONV_WIDTH),
            w_out[l].astype(_BF16), row(mix_post_g, D_MODEL),
            _ffn_weights(ffn2_pre_g[l], ffn2_w_gu[l], ffn2_w_down[l], ffn2_post_g[l]),
            row(final_g, D_MODEL))


def kernel(x_prompt, x_sample, ffn1_pre_g, ffn1_w_gu, ffn1_w_down, ffn1_post_g, mix_pre_g, w_in, conv_w, conv_b, conv_ln_g, conv_ln_b, w_out, mix_post_g, ffn2_pre_g, ffn2_w_gu, ffn2_w_down, ffn2_post_g, final_g):
    params = (ffn1_pre_g, ffn1_w_gu, ffn1_w_down, ffn1_post_g, mix_pre_g, w_in, conv_w, conv_b, conv_ln_g,
              conv_ln_b, w_out, mix_post_g, ffn2_pre_g, ffn2_w_gu, ffn2_w_down, ffn2_post_g, final_g)
    layers = [_layer_weights(l, *params) for l in range(ffn1_pre_g.shape[0])]
    outs = []
    for x in (x_prompt, x_sample):
        for layer in layers:
            x = _layer(x, *layer)
        outs.append(x)
    return tuple(outs)
```

```python
import functools
import math

import jax
import jax.numpy as jnp
from jax import lax
from jax.experimental import pallas as pl
from jax.experimental.pallas import tpu as pltpu

D_MODEL = 1024
D_FF = 2816
ATTN_WIDTH = 512
CONV_WIDTH = 512
HEAD_DIM = 64
N_HEADS = ATTN_WIDTH // HEAD_DIM
HEAD_PAIRS = N_HEADS // 2
DILATED_BRANCHES = ((128, 1), (512, 4), (2048, 16))
DILATIONS = tuple(d for _, d in DILATED_BRANCHES)
HALF = 64
CONV_KERNEL = 31
CONV_PAD = CONV_KERNEL // 2
FFN_RESIDUAL_WEIGHT = 0.5
RMS_EPS = 1e-6
LN_EPS = 1e-5
NEG_INF = -1e30
LOG2_E = math.log2(math.e)
LN_2 = math.log(2.0)

LANES = 128
SUBLANES = 8
LANE_CHUNKS = ATTN_WIDTH // LANES
Q_SUB = 2 * HALF
K_WIN = Q_SUB + 2 * HALF
FF_CHUNK = 256
N_FF_CHUNKS = D_FF // FF_CHUNK
ATTN_STEP_POSITIONS = 2048
FFN_TILE = 512
FFN_FINAL_TILE = 1024
TAIL_TILE = 512
CONV_HALO = 16
CONV_TILE = 256
CONV_ROWS = 128
CONV_FIRST = CONV_HALO - CONV_PAD
CONV_SPAN = -(-(CONV_FIRST + CONV_KERNEL - 1) // SUBLANES) * SUBLANES
VMEM_LIMIT_BYTES = 56 * 1024 * 1024

assert all(w // (2 * d) == HALF for w, d in DILATED_BRANCHES)
assert DILATIONS[0] == 1 and D_FF % FF_CHUNK == 0
assert math.frexp(FFN_RESIDUAL_WEIGHT)[0] == 0.5

_BF16 = jnp.bfloat16
_F32 = jnp.float32


def _rms(x, g):
    ms = jnp.mean(x * x, axis=-1, keepdims=True)
    return x * lax.rsqrt(ms + RMS_EPS) * g


def _whole_vmem():
    return pl.BlockSpec(memory_space=pltpu.VMEM)


def _swiglu(h_scr, a_scr, wgu, wd):
    for c in range(N_FF_CHUNKS):
        h = h_scr[...]
        gate = jnp.dot(h, wgu[:, c * FF_CHUNK:(c + 1) * FF_CHUNK], preferred_element_type=_F32)
        up = jnp.dot(h, wgu[:, D_FF + c * FF_CHUNK:D_FF + (c + 1) * FF_CHUNK], preferred_element_type=_F32)
        a_scr[:, c * FF_CHUNK:(c + 1) * FF_CHUNK] = (gate * jax.nn.sigmoid(gate) * up).astype(_BF16)
    return jnp.dot(a_scr[...], wd[...], preferred_element_type=_F32)


def _store_views(proj, view_refs, lane_scr):
    rows = proj.shape[0]
    view_refs[0][...] = proj.astype(_BF16)
    for c in range(LANE_CHUNKS):
        lane_scr[c] = proj[:, c * LANES:(c + 1) * LANES]
    for dilation, ref in zip(DILATIONS[1:], view_refs[1:]):
        for r in range(dilation):
            for c in range(LANE_CHUNKS):
                col = r * ATTN_WIDTH + c * LANES
                ref[:, col:col + LANES] = lane_scr[c, pl.ds(r, rows // dilation, stride=dilation), :].astype(_BF16)


def _ffn_proj_kernel(x_ref, pre_g, wgu, wd, post_g, mix_g, wq, wk, wv, wa, wb,
                     x1_ref, q1_ref, q4_ref, q16_ref, k1_ref, k4_ref, k16_ref, v1_ref, v4_ref, v16_ref, glu_ref,
                     h_scr, a_scr, lane_scr):
    h_scr[...] = _rms(x_ref[...], pre_g[...]).astype(_BF16)
    x1 = x_ref[...] + _rms(_swiglu(h_scr, a_scr, wgu, wd), post_g[...])
    x1_ref[...] = x1
    h_scr[...] = _rms(x1, mix_g[...]).astype(_BF16)
    q = jnp.dot(h_scr[...], wq[...], preferred_element_type=_F32) * (HEAD_DIM ** -0.5 * LOG2_E)
    _store_views(q, (q1_ref, q4_ref, q16_ref), lane_scr)
    k = jnp.dot(h_scr[...], wk[...], preferred_element_type=_F32)
    _store_views(k, (k1_ref, k4_ref, k16_ref), lane_scr)
    v = jnp.dot(h_scr[...], wv[...], preferred_element_type=_F32)
    _store_views(v, (v1_ref, v4_ref, v16_ref), lane_scr)
    ga = jnp.dot(h_scr[...], wa[...], preferred_element_type=_F32)
    gb = jnp.dot(h_scr[...], wb[...], preferred_element_type=_F32)
    glu_ref[...] = ga * jax.nn.sigmoid(gb)


def _ffn_final_kernel(x_ref, pre_g, wgu, wd, post_g, final_g, y_ref, h_scr, a_scr):
    h_scr[...] = _rms(x_ref[...], pre_g[...]).astype(_BF16)
    x3 = x_ref[...] + _rms(_swiglu(h_scr, a_scr, wgu, wd), post_g[...])
    y_ref[...] = _rms(x3, final_g[...])


def _ffn_call(body, name, tm, x, weights, outs, extra_scratch=()):
    n_tokens = x.shape[0]
    row_spec = lambda d, w: pl.BlockSpec((tm // d, d * w), lambda i: (i, 0))
    return pl.pallas_call(
        body,
        name=name,
        grid=(n_tokens // tm,),
        in_specs=[row_spec(1, D_MODEL)] + [_whole_vmem() for _ in weights],
        out_specs=[row_spec(d, w) for d, w, _ in outs],
        out_shape=[jax.ShapeDtypeStruct((n_tokens // d, d * w), dt) for d, w, dt in outs],
        scratch_shapes=[pltpu.VMEM((tm, D_MODEL), _BF16), pltpu.VMEM((tm, D_FF), _BF16), *extra_scratch],
        compiler_params=pltpu.CompilerParams(dimension_semantics=("arbitrary",),
                                             vmem_limit_bytes=VMEM_LIMIT_BYTES),
    )(x, *weights)


def _attn_kernel(q_ref, kp_ref, k_ref, kn_ref, vp_ref, v_ref, vn_ref, o_ref, lse_ref, bias_scr,
                 *, dilation, rows, residues):
    j = pl.program_id(2)
    n_sub = rows // Q_SUB

    @pl.when((pl.program_id(0) == 0) & (pl.program_id(1) == 0) & (j == 0))
    def _():
        row = lax.broadcasted_iota(jnp.int32, (2 * Q_SUB, K_WIN), 0)
        col = lax.broadcasted_iota(jnp.int32, (2 * Q_SUB, K_WIN), 1)
        rel = jnp.abs(col - HALF - (row & (Q_SUB - 1)))
        dist = rel.astype(_F32)
        for p in range(HEAD_PAIRS):
            slope_even = dilation * 2.0 ** (-8.0 * (2 * p + 1) / N_HEADS)
            slope_odd = dilation * 2.0 ** (-8.0 * (2 * p + 2) / N_HEADS)
            slope = jnp.where(row < Q_SUB, slope_even, slope_odd)
            bias_scr[p] = jnp.where(rel <= HALF, -slope * LOG2_E * dist, NEG_INF)

    lane = lax.broadcasted_iota(jnp.int32, (Q_SUB, LANES), 1)
    even_lanes = lane < HEAD_DIM
    col = lax.broadcasted_iota(jnp.int32, (2 * Q_SUB, K_WIN), 1)
    first_valid_col = jnp.where(j == 0, HALF, 0)
    last_valid_col = jnp.where(j == pl.num_programs(2) - 1, K_WIN - HALF, K_WIN)
    ones = jnp.ones((K_WIN, LANES), _BF16)

    def window(prev_ref, main_ref, next_ref, i, lanes):
        parts = []
        lo = i * Q_SUB - HALF
        hi = lo + K_WIN
        if lo < 0:
            parts.append(prev_ref[0, :, lanes])
            lo = 0
        parts.append(main_ref[0, lo:min(hi, rows), lanes])
        if hi > rows:
            parts.append(next_ref[0, :, lanes])
        return parts[0] if len(parts) == 1 else jnp.concatenate(parts, axis=0)

    for res, i, p in ((res, i, p) for res in range(residues) for i in range(n_sub) for p in range(HEAD_PAIRS)):
        q_rows = slice(i * Q_SUB, (i + 1) * Q_SUB)
        lanes = slice(res * ATTN_WIDTH + p * LANES, res * ATTN_WIDTH + (p + 1) * LANES)
        q2 = q_ref[0, q_rows, lanes]
        zero = jnp.zeros_like(q2)
        q_stack = jnp.concatenate([jnp.where(even_lanes, q2, zero), jnp.where(even_lanes, zero, q2)], axis=0)
        kw = window(kp_ref, k_ref, kn_ref, i, lanes)
        s = lax.dot_general(q_stack, kw, (((1,), (1,)), ((), ())), preferred_element_type=_F32)
        s = s + bias_scr[p]
        if i == 0:
            s = jnp.where(col < first_valid_col, NEG_INF, s)
        if i == n_sub - 1:
            s = jnp.where(col >= last_valid_col, NEG_INF, s)
        m = jnp.max(s, axis=-1, keepdims=True)
        e = jnp.exp2(s - m).astype(_BF16)
        vw = window(vp_ref, v_ref, vn_ref, i, lanes)
        r = jnp.dot(e, jnp.concatenate([vw, ones], axis=1), preferred_element_type=_F32)
        acc = jnp.where(even_lanes, r[:Q_SUB, :LANES], r[Q_SUB:, :LANES])
        den = jnp.where(even_lanes, r[:Q_SUB, LANES:], r[Q_SUB:, LANES:])
        mx = jnp.where(even_lanes, m[:Q_SUB], m[Q_SUB:])
        o_ref[0, q_rows, lanes] = acc / den
        lse_ref[0, q_rows, lanes] = mx * LN_2 + jnp.log(den)


def _attn_branch(q, k, v, dilation):
    batch, sub_len, _ = q.shape
    rows = min(sub_len, ATTN_STEP_POSITIONS)
    residues = min(dilation, ATTN_STEP_POSITIONS // rows)
    width = residues * ATTN_WIDTH
    halo_per_tile = rows // HALF
    last_halo = sub_len // HALF - 1
    main = pl.BlockSpec((1, rows, width), lambda b, r, j: (b, j, r))
    prev = pl.BlockSpec((1, HALF, width), lambda b, r, j: (b, jnp.maximum(j * halo_per_tile - 1, 0), r))
    nxt = pl.BlockSpec((1, HALF, width), lambda b, r, j: (b, jnp.minimum((j + 1) * halo_per_tile, last_halo), r))
    out_shape = jax.ShapeDtypeStruct((batch, sub_len, dilation * ATTN_WIDTH), _F32)
    return pl.pallas_call(
        functools.partial(_attn_kernel, dilation=dilation, rows=rows, residues=residues),
        name=f"attn_dilation{dilation}",
        grid=(batch, dilation // residues, sub_len // rows),
        in_specs=[main, prev, main, nxt, prev, main, nxt],
        out_specs=[main, main],
        out_shape=[out_shape, out_shape],
        scratch_shapes=[pltpu.VMEM((HEAD_PAIRS, 2 * Q_SUB, K_WIN), _F32)],
        compiler_params=pltpu.CompilerParams(dimension_semantics=("arbitrary",) * 3,
                                             vmem_limit_bytes=VMEM_LIMIT_BYTES),
    )(q, k, k, k, v, v, v)


def _mixer_tail_kernel(x1_ref, o1, o4, o16, l1, l4, l16, gp_ref, g_ref, gn_ref,
                       conv_w, conv_b, ln_g, ln_b, w_out, post_g, x2_ref,
                       nat_scr, g_scr, shift_scr, m_scr, *, rows):
    i = pl.program_id(1)

    for c in range(LANE_CHUNKS):
        lanes = slice(c * LANES, (c + 1) * LANES)
        for a, ref in enumerate((o4, l4, o16, l16)):
            dilation = DILATIONS[1 + a // 2]
            for r in range(dilation):
                col = r * ATTN_WIDTH + c * LANES
                nat_scr[a, pl.ds(r, rows // dilation, stride=dilation), :] = ref[0, :, col:col + LANES]
        outs = (o1[0, :, lanes], nat_scr[0], nat_scr[2])
        lses = (l1[0, :, lanes], nat_scr[1], nat_scr[3])
        top = jnp.maximum(jnp.maximum(lses[0], lses[1]), lses[2])
        w = [jnp.exp(l - top) for l in lses]
        attn = (w[0] * outs[0] + w[1] * outs[1] + w[2] * outs[2]) / (w[0] + w[1] + w[2])
        m_scr[:, lanes] = attn.astype(_BF16)

    g_scr[:CONV_HALO] = jnp.where(i > 0, gp_ref[0], 0.0)
    g_scr[CONV_HALO:CONV_HALO + rows] = g_ref[0]
    g_scr[CONV_HALO + rows:CONV_HALO + rows + CONV_HALO] = jnp.where(i < pl.num_programs(1) - 1, gn_ref[0], 0.0)
    for t0 in range(0, rows, CONV_TILE):
        for b in range(SUBLANES):
            shift_scr[b] = g_scr[t0 + b:t0 + b + CONV_TILE + CONV_SPAN - SUBLANES, :]
        for base in range(0, CONV_TILE, CONV_ROWS):
            acc = jnp.broadcast_to(conv_b[...], (CONV_ROWS, CONV_WIDTH))
            for t in range(CONV_KERNEL):
                off = CONV_FIRST + t
                start = base + off // SUBLANES * SUBLANES
                acc = acc + conv_w[t:t + 1, :] * shift_scr[off % SUBLANES, start:start + CONV_ROWS, :]
            mu = jnp.mean(acc, axis=-1, keepdims=True)
            cen = acc - mu
            var = jnp.mean(cen * cen, axis=-1, keepdims=True)
            y = cen * lax.rsqrt(var + LN_EPS) * ln_g[...] + ln_b[...]
            m_scr[t0 + base:t0 + base + CONV_ROWS, ATTN_WIDTH:] = (y * jax.nn.sigmoid(y)).astype(_BF16)

    mixed = jnp.dot(m_scr[...], w_out[...], preferred_element_type=_F32)
    x2_ref[0] = x1_ref[0] + _rms(mixed, post_g[...])


def _mixer_tail(x1, branch_outs, glu, conv_w, conv_b, ln_g, ln_b, w_out, post_g):
    batch, seq, _ = x1.shape
    rows = TAIL_TILE
    halo_per_tile = rows // CONV_HALO
    last_halo = seq // CONV_HALO - 1
    tile = lambda w, d=1: pl.BlockSpec((1, rows // d, d * w), lambda b, i: (b, i, 0))
    prev = pl.BlockSpec((1, CONV_HALO, CONV_WIDTH), lambda b, i: (b, jnp.maximum(i * halo_per_tile - 1, 0), 0))
    nxt = pl.BlockSpec((1, CONV_HALO, CONV_WIDTH),
                       lambda b, i: (b, jnp.minimum((i + 1) * halo_per_tile, last_halo), 0))
    outs = [o for o, _ in branch_outs]
    lses = [l for _, l in branch_outs]
    views = [tile(ATTN_WIDTH, d) for d in DILATIONS]
    weights = (conv_w, conv_b, ln_g, ln_b, w_out, post_g)
    return pl.pallas_call(
        functools.partial(_mixer_tail_kernel, rows=rows),
        name="mixer_tail",
        grid=(batch, seq // rows),
        in_specs=[tile(D_MODEL)] + views + views + [prev, tile(CONV_WIDTH), nxt]
                 + [_whole_vmem() for _ in weights],
        out_specs=tile(D_MODEL),
        out_shape=jax.ShapeDtypeStruct((batch, seq, D_MODEL), _F32),
        scratch_shapes=[pltpu.VMEM((4, rows, LANES), _F32),
                        pltpu.VMEM((rows + CONV_SPAN, CONV_WIDTH), _F32),
                        pltpu.VMEM((SUBLANES, CONV_TILE + CONV_SPAN - SUBLANES, CONV_WIDTH), _F32),
                        pltpu.VMEM((rows, ATTN_WIDTH + CONV_WIDTH), _BF16)],
        compiler_params=pltpu.CompilerParams(dimension_semantics=("arbitrary",) * 2,
                                             vmem_limit_bytes=VMEM_LIMIT_BYTES),
    )(x1, *outs, *lses, glu, glu, glu, *weights)


def _ffn_weights(pre_g, w_gu, w_down, post_g):
    post_g = post_g * FFN_RESIDUAL_WEIGHT
    return (pre_g.reshape(1, D_MODEL), w_gu.astype(_BF16), w_down.astype(_BF16), post_g.reshape(1, D_MODEL))


def _layer(x, ffn1, mix_g, w_in_parts, conv_w, conv_b, ln_g, ln_b, w_out, mix_post_g, ffn2, final_g):
    batch, seq, _ = x.shape
    tokens = batch * seq
    qkv_views = [(d, ATTN_WIDTH, _BF16) for _ in range(3) for d in DILATIONS]
    x1, *qkv, glu = _ffn_call(
        _ffn_proj_kernel, "ffn_proj", FFN_TILE, x.reshape(tokens, D_MODEL), ffn1 + (mix_g,) + w_in_parts,
        [(1, D_MODEL, _F32)] + qkv_views + [(1, CONV_WIDTH, _F32)],
        extra_scratch=[pltpu.VMEM((LANE_CHUNKS, FFN_TILE, LANES), _F32)])
    per_batch = lambda a: a.reshape(batch, a.shape[0] // batch, a.shape[1])
    n = len(DILATIONS)
    branch_outs = [_attn_branch(per_batch(qkv[b]), per_batch(qkv[n + b]), per_batch(qkv[2 * n + b]), d)
                   for b, d in enumerate(DILATIONS)]
    x2 = _mixer_tail(per_batch(x1), branch_outs, per_batch(glu), conv_w, conv_b, ln_g, ln_b, w_out, mix_post_g)
    (y,) = _ffn_call(_ffn_final_kernel, "ffn_final", FFN_FINAL_TILE, x2.reshape(tokens, D_MODEL),
                     ffn2 + (final_g,), [(1, D_MODEL, _F32)])
    return y.reshape(batch, seq, D_MODEL)


def _layer_weights(l, ffn1_pre_g, ffn1_w_gu, ffn1_w_down, ffn1_post_g, mix_pre_g, w_in, conv_w, conv_b,
                   conv_ln_g, conv_ln_b, w_out, mix_post_g, ffn2_pre_g, ffn2_w_gu, ffn2_w_down, ffn2_post_g,
                   final_g):
    w_in_l = w_in[l].astype(_BF16)
    splits = (0, ATTN_WIDTH, 2 * ATTN_WIDTH, 3 * ATTN_WIDTH, 3 * ATTN_WIDTH + CONV_WIDTH,
              3 * ATTN_WIDTH + 2 * CONV_WIDTH)
    w_in_parts = tuple(w_in_l[:, a:b] for a, b in zip(splits[:-1], splits[1:]))
    row = lambda g, n: g[l].reshape(1, n)
    return (_ffn_weights(ffn1_pre_g[l], ffn1_w_gu[l], ffn1_w_down[l], ffn1_post_g[l]),
            row(mix_pre_g, D_MODEL), w_in_parts,
            conv_w[l].reshape(CONV_KERNEL, CONV_WIDTH), row(conv_b, CONV_WIDTH),
            row(conv_ln_g, CONV_WIDTH), row(conv_ln_b, CONV_WIDTH),
            w_out[l].astype(_BF16), row(mix_post_g, D_MODEL),
            _ffn_weights(ffn2_pre_g[l], ffn2_w_gu[l], ffn2_w_down[l], ffn2_post_g[l]),
            row(final_g, D_MODEL))


def kernel(x_prompt, x_sample, ffn1_pre_g, ffn1_w_gu, ffn1_w_down, ffn1_post_g, mix_pre_g, w_in, conv_w, conv_b, conv_ln_g, conv_ln_b, w_out, mix_post_g, ffn2_pre_g, ffn2_w_gu, ffn2_w_down, ffn2_post_g, final_g):
    params = (ffn1_pre_g, ffn1_w_gu, ffn1_w_down, ffn1_post_g, mix_pre_g, w_in, conv_w, conv_b, conv_ln_g,
              conv_ln_b, w_out, mix_post_g, ffn2_pre_g, ffn2_w_gu, ffn2_w_down, ffn2_post_g, final_g)
    layers = [_layer_weights(l, *params) for l in range(ffn1_pre_g.shape[0])]
    outs = []
    for x in (x_prompt, x_sample):
        for layer in layers:
            x = _layer(x, *layer)
        outs.append(x)
    return tuple(outs)
```

```python
import functools
import math

import jax
import jax.numpy as jnp
from jax import lax
from jax.experimental import pallas as pl
from jax.experimental.pallas import tpu as pltpu

D_MODEL = 1024
D_FF = 2816
ATTN_WIDTH = 512
CONV_WIDTH = 512
HEAD_DIM = 64
N_HEADS = ATTN_WIDTH // HEAD_DIM
HEAD_PAIRS = N_HEADS // 2
DILATED_BRANCHES = ((128, 1), (512, 4), (2048, 16))
DILATIONS = tuple(d for _, d in DILATED_BRANCHES)
HALF = 64
CONV_KERNEL = 31
CONV_PAD = CONV_KERNEL // 2
FFN_RESIDUAL_WEIGHT = 0.5
RMS_EPS = 1e-6
LN_EPS = 1e-5
NEG_INF = -1e30
LOG2_E = math.log2(math.e)
LN_2 = math.log(2.0)

LANES = 128
SUBLANES = 8
LANE_CHUNKS = ATTN_WIDTH // LANES
HEAD_LANES = LANES // N_HEADS
Q_SUB = 2 * HALF
K_WIN = Q_SUB + 2 * HALF
FF_CHUNK = 256
N_FF_CHUNKS = D_FF // FF_CHUNK
ATTN_STEP_POSITIONS = 2048
FFN_TILE = 512
FFN_FINAL_TILE = 1024
TAIL_TILE = 512
CONV_HALO = 16
CONV_TILE = 256
CONV_ROWS = 128
CONV_FIRST = CONV_HALO - CONV_PAD
CONV_SPAN = -(-(CONV_FIRST + CONV_KERNEL - 1) // SUBLANES) * SUBLANES
VMEM_LIMIT_BYTES = 56 * 1024 * 1024

assert all(w // (2 * d) == HALF for w, d in DILATED_BRANCHES)
assert DILATIONS[0] == 1 and D_FF % FF_CHUNK == 0
assert math.frexp(FFN_RESIDUAL_WEIGHT)[0] == 0.5

_BF16 = jnp.bfloat16
_F32 = jnp.float32


def _rms(x, g):
    ms = jnp.mean(x * x, axis=-1, keepdims=True)
    return x * lax.rsqrt(ms + RMS_EPS) * g


def _whole_vmem():
    return pl.BlockSpec(memory_space=pltpu.VMEM)


def _swiglu(h_scr, a_scr, wgu, wd):
    for c in range(N_FF_CHUNKS):
        h = h_scr[...]
        gate = jnp.dot(h, wgu[:, c * FF_CHUNK:(c + 1) * FF_CHUNK], preferred_element_type=_F32)
        up = jnp.dot(h, wgu[:, D_FF + c * FF_CHUNK:D_FF + (c + 1) * FF_CHUNK], preferred_element_type=_F32)
        a_scr[:, c * FF_CHUNK:(c + 1) * FF_CHUNK] = (gate * jax.nn.sigmoid(gate) * up).astype(_BF16)
    return jnp.dot(a_scr[...], wd[...], preferred_element_type=_F32)


def _store_views(proj, view_refs, lane_scr):
    rows = proj.shape[0]
    view_refs[0][...] = proj.astype(_BF16)
    for c in range(LANE_CHUNKS):
        lane_scr[c] = proj[:, c * LANES:(c + 1) * LANES]
    for dilation, ref in zip(DILATIONS[1:], view_refs[1:]):
        for r in range(dilation):
            for c in range(LANE_CHUNKS):
                col = r * ATTN_WIDTH + c * LANES
                ref[:, col:col + LANES] = lane_scr[c, pl.ds(r, rows // dilation, stride=dilation), :].astype(_BF16)


def _ffn_proj_kernel(x_ref, pre_g, wgu, wd, post_g, mix_g, wq, wk, wv, wa, wb,
                     x1_ref, q1_ref, q4_ref, q16_ref, k1_ref, k4_ref, k16_ref, v1_ref, v4_ref, v16_ref, glu_ref,
                     h_scr, a_scr, lane_scr):
    h_scr[...] = _rms(x_ref[...], pre_g[...]).astype(_BF16)
    x1 = x_ref[...] + _rms(_swiglu(h_scr, a_scr, wgu, wd), post_g[...])
    x1_ref[...] = x1
    h_scr[...] = _rms(x1, mix_g[...]).astype(_BF16)
    q = jnp.dot(h_scr[...], wq[...], preferred_element_type=_F32) * (HEAD_DIM ** -0.5 * LOG2_E)
    _store_views(q, (q1_ref, q4_ref, q16_ref), lane_scr)
    k = jnp.dot(h_scr[...], wk[...], preferred_element_type=_F32)
    _store_views(k, (k1_ref, k4_ref, k16_ref), lane_scr)
    v = jnp.dot(h_scr[...], wv[...], preferred_element_type=_F32)
    _store_views(v, (v1_ref, v4_ref, v16_ref), lane_scr)
    ga = jnp.dot(h_scr[...], wa[...], preferred_element_type=_F32)
    gb = jnp.dot(h_scr[...], wb[...], preferred_element_type=_F32)
    glu_ref[...] = ga * jax.nn.sigmoid(gb)


def _ffn_final_kernel(x_ref, pre_g, wgu, wd, post_g, final_g, y_ref, h_scr, a_scr):
    h_scr[...] = _rms(x_ref[...], pre_g[...]).astype(_BF16)
    x3 = x_ref[...] + _rms(_swiglu(h_scr, a_scr, wgu, wd), post_g[...])
    y_ref[...] = _rms(x3, final_g[...])


def _ffn_call(body, name, tm, x, weights, outs, extra_scratch=()):
    n_tokens = x.shape[0]
    row_spec = lambda d, w: pl.BlockSpec((tm // d, d * w), lambda i: (i, 0))
    return pl.pallas_call(
        body,
        name=name,
        grid=(n_tokens // tm,),
        in_specs=[row_spec(1, D_MODEL)] + [_whole_vmem() for _ in weights],
        out_specs=[row_spec(d, w) for d, w, _ in outs],
        out_shape=[jax.ShapeDtypeStruct((n_tokens // d, d * w), dt) for d, w, dt in outs],
        scratch_shapes=[pltpu.VMEM((tm, D_MODEL), _BF16), pltpu.VMEM((tm, D_FF), _BF16), *extra_scratch],
        compiler_params=pltpu.CompilerParams(dimension_semantics=("arbitrary",),
                                             vmem_limit_bytes=VMEM_LIMIT_BYTES),
    )(x, *weights)


def _attn_kernel(q_ref, kp_ref, k_ref, kn_ref, vp_ref, v_ref, vn_ref, o_ref, lse_ref, bias_scr,
                 *, dilation, rows, residues):
    j = pl.program_id(2)
    n_sub = rows // Q_SUB

    @pl.when((pl.program_id(0) == 0) & (pl.program_id(1) == 0) & (j == 0))
    def _():
        row = lax.broadcasted_iota(jnp.int32, (2 * Q_SUB, K_WIN), 0)
        col = lax.broadcasted_iota(jnp.int32, (2 * Q_SUB, K_WIN), 1)
        rel = jnp.abs(col - HALF - (row & (Q_SUB - 1)))
        dist = rel.astype(_F32)
        for p in range(HEAD_PAIRS):
            slope_even = dilation * 2.0 ** (-8.0 * (2 * p + 1) / N_HEADS)
            slope_odd = dilation * 2.0 ** (-8.0 * (2 * p + 2) / N_HEADS)
            slope = jnp.where(row < Q_SUB, slope_even, slope_odd)
            bias_scr[p] = jnp.where(rel <= HALF, -slope * LOG2_E * dist, NEG_INF)

    lane = lax.broadcasted_iota(jnp.int32, (Q_SUB, LANES), 1)
    even_lanes = lane < HEAD_DIM
    col = lax.broadcasted_iota(jnp.int32, (2 * Q_SUB, K_WIN), 1)
    first_valid_col = jnp.where(j == 0, HALF, 0)
    last_valid_col = jnp.where(j == pl.num_programs(2) - 1, K_WIN - HALF, K_WIN)
    ones = jnp.ones((K_WIN, LANES), _BF16)

    def window(prev_ref, main_ref, next_ref, i, lanes):
        parts = []
        lo = i * Q_SUB - HALF
        hi = lo + K_WIN
        if lo < 0:
            parts.append(prev_ref[0, :, lanes])
            lo = 0
        parts.append(main_ref[0, lo:min(hi, rows), lanes])
        if hi > rows:
            parts.append(next_ref[0, :, lanes])
        return parts[0] if len(parts) == 1 else jnp.concatenate(parts, axis=0)

    pair_lane = (lax.broadcasted_iota(jnp.int32, (Q_SUB, LANES), 1) % HEAD_DIM) // HEAD_LANES
    for res, i in ((res, i) for res in range(residues) for i in range(n_sub)):
        q_rows = slice(i * Q_SUB, (i + 1) * Q_SUB)
        packed = None
        for p in range(HEAD_PAIRS):
            lanes = slice(res * ATTN_WIDTH + p * LANES, res * ATTN_WIDTH + (p + 1) * LANES)
            q2 = q_ref[0, q_rows, lanes]
            zero = jnp.zeros_like(q2)
            q_stack = jnp.concatenate([jnp.where(even_lanes, q2, zero), jnp.where(even_lanes, zero, q2)], axis=0)
            kw = window(kp_ref, k_ref, kn_ref, i, lanes)
            s = lax.dot_general(q_stack, kw, (((1,), (1,)), ((), ())), preferred_element_type=_F32)
            s = s + bias_scr[p]
            if i == 0:
                s = jnp.where(col < first_valid_col, NEG_INF, s)
            if i == n_sub - 1:
                s = jnp.where(col >= last_valid_col, NEG_INF, s)
            m = jnp.max(s, axis=-1, keepdims=True)
            e = jnp.exp2(s - m).astype(_BF16)
            vw = window(vp_ref, v_ref, vn_ref, i, lanes)
            r = jnp.dot(e, jnp.concatenate([vw, ones], axis=1), preferred_element_type=_F32)
            acc = jnp.where(even_lanes, r[:Q_SUB, :LANES], r[Q_SUB:, :LANES])
            den = jnp.where(even_lanes, r[:Q_SUB, LANES:], r[Q_SUB:, LANES:])
            mx = jnp.where(even_lanes, m[:Q_SUB], m[Q_SUB:])
            o_ref[0, q_rows, lanes] = acc / den
            lse = mx * LN_2 + jnp.log(den)
            packed = lse if packed is None else jnp.where(pair_lane == p, lse, packed)
        lse_ref[0, q_rows, res * LANES:(res + 1) * LANES] = packed


def _attn_branch(q, k, v, dilation):
    batch, sub_len, _ = q.shape
    rows = min(sub_len, ATTN_STEP_POSITIONS)
    residues = min(dilation, ATTN_STEP_POSITIONS // rows)
    width = residues * ATTN_WIDTH
    halo_per_tile = rows // HALF
    last_halo = sub_len // HALF - 1
    main = pl.BlockSpec((1, rows, width), lambda b, r, j: (b, j, r))
    prev = pl.BlockSpec((1, HALF, width), lambda b, r, j: (b, jnp.maximum(j * halo_per_tile - 1, 0), r))
    nxt = pl.BlockSpec((1, HALF, width), lambda b, r, j: (b, jnp.minimum((j + 1) * halo_per_tile, last_halo), r))
    lse_block = pl.BlockSpec((1, rows, residues * LANES), lambda b, r, j: (b, j, r))
    out_shapes = [jax.ShapeDtypeStruct((batch, sub_len, dilation * w), _F32) for w in (ATTN_WIDTH, LANES)]
    return pl.pallas_call(
        functools.partial(_attn_kernel, dilation=dilation, rows=rows, residues=residues),
        name=f"attn_dilation{dilation}",
        grid=(batch, dilation // residues, sub_len // rows),
        in_specs=[main, prev, main, nxt, prev, main, nxt],
        out_specs=[main, lse_block],
        out_shape=out_shapes,
        scratch_shapes=[pltpu.VMEM((HEAD_PAIRS, 2 * Q_SUB, K_WIN), _F32)],
        compiler_params=pltpu.CompilerParams(dimension_semantics=("arbitrary",) * 3,
                                             vmem_limit_bytes=VMEM_LIMIT_BYTES),
    )(q, k, k, k, v, v, v)


def _mixer_tail_kernel(x1_ref, o1, o4, o16, l1, l4, l16, gp_ref, g_ref, gn_ref,
                       conv_w, conv_b, ln_g, ln_b, w_out, post_g, x2_ref,
                       nat_scr, lse_scr, expand_scr, g_scr, shift_scr, m_scr, *, rows):
    i = pl.program_id(1)

    @pl.when((pl.program_id(0) == 0) & (i == 0))
    def _():
        src = lax.broadcasted_iota(jnp.int32, (2 * LANES, ATTN_WIDTH), 0) % LANES
        head = lax.broadcasted_iota(jnp.int32, (2 * LANES, ATTN_WIDTH), 1) // HEAD_DIM
        packed_lane = (head % 2) * HEAD_DIM + (head // 2) * HEAD_LANES
        expand_scr[...] = jnp.where(src == packed_lane, 1.0, 0.0).astype(_BF16)

    for a, ref in enumerate((l4, l16)):
        dilation = DILATIONS[1 + a]
        for r in range(dilation):
            lse_scr[a, pl.ds(r, rows // dilation, stride=dilation), :] = ref[0, :, r * LANES:(r + 1) * LANES]
    lses = (l1[0], lse_scr[0], lse_scr[1])
    top = jnp.maximum(jnp.maximum(lses[0], lses[1]), lses[2])
    w = [jnp.exp(l - top) for l in lses]
    total = w[0] + w[1] + w[2]
    weights = []
    for wb in w:
        wn = wb / total
        hi = wn.astype(_BF16)
        lo = (wn - hi.astype(_F32)).astype(_BF16)
        weights.append(jnp.dot(jnp.concatenate([hi, lo], axis=1), expand_scr[...], preferred_element_type=_F32))

    for c in range(LANE_CHUNKS):
        lanes = slice(c * LANES, (c + 1) * LANES)
        for a, ref in enumerate((o4, o16)):
            dilation = DILATIONS[1 + a]
            for r in range(dilation):
                col = r * ATTN_WIDTH + c * LANES
                nat_scr[a, pl.ds(r, rows // dilation, stride=dilation), :] = ref[0, :, col:col + LANES]
        attn = (weights[0][:, lanes] * o1[0, :, lanes] + weights[1][:, lanes] * nat_scr[0]
                + weights[2][:, lanes] * nat_scr[1])
        m_scr[:, lanes] = attn.astype(_BF16)

    g_scr[:CONV_HALO] = jnp.where(i > 0, gp_ref[0], 0.0)
    g_scr[CONV_HALO:CONV_HALO + rows] = g_ref[0]
    g_scr[CONV_HALO + rows:CONV_HALO + rows + CONV_HALO] = jnp.where(i < pl.num_programs(1) - 1, gn_ref[0], 0.0)
    for t0 in range(0, rows, CONV_TILE):
        for b in range(SUBLANES):
            shift_scr[b] = g_scr[t0 + b:t0 + b + CONV_TILE + CONV_SPAN - SUBLANES, :]
        for base in range(0, CONV_TILE, CONV_ROWS):
            acc = jnp.broadcast_to(conv_b[...], (CONV_ROWS, CONV_WIDTH))
            for t in range(CONV_KERNEL):
                off = CONV_FIRST + t
                start = base + off // SUBLANES * SUBLANES
                acc = acc + conv_w[t:t + 1, :] * shift_scr[off % SUBLANES, start:start + CONV_ROWS, :]
            mu = jnp.mean(acc, axis=-1, keepdims=True)
            cen = acc - mu
            var = jnp.mean(cen * cen, axis=-1, keepdims=True)
            y = cen * lax.rsqrt(var + LN_EPS) * ln_g[...] + ln_b[...]
            m_scr[t0 + base:t0 + base + CONV_ROWS, ATTN_WIDTH:] = (y * jax.nn.sigmoid(y)).astype(_BF16)

    mixed = jnp.dot(m_scr[...], w_out[...], preferred_element_type=_F32)
    x2_ref[0] = x1_ref[0] + _rms(mixed, post_g[...])


def _mixer_tail(x1, branch_outs, glu, conv_w, conv_b, ln_g, ln_b, w_out, post_g):
    batch, seq, _ = x1.shape
    rows = TAIL_TILE
    halo_per_tile = rows // CONV_HALO
    last_halo = seq // CONV_HALO - 1
    tile = lambda w, d=1: pl.BlockSpec((1, rows // d, d * w), lambda b, i: (b, i, 0))
    prev = pl.BlockSpec((1, CONV_HALO, CONV_WIDTH), lambda b, i: (b, jnp.maximum(i * halo_per_tile - 1, 0), 0))
    nxt = pl.BlockSpec((1, CONV_HALO, CONV_WIDTH),
                       lambda b, i: (b, jnp.minimum((i + 1) * halo_per_tile, last_halo), 0))
    outs = [o for o, _ in branch_outs]
    lses = [l for _, l in branch_outs]
    views = lambda w: [tile(w, d) for d in DILATIONS]
    weights = (conv_w, conv_b, ln_g, ln_b, w_out, post_g)
    return pl.pallas_call(
        functools.partial(_mixer_tail_kernel, rows=rows),
        name="mixer_tail",
        grid=(batch, seq // rows),
        in_specs=[tile(D_MODEL)] + views(ATTN_WIDTH) + views(LANES) + [prev, tile(CONV_WIDTH), nxt]
                 + [_whole_vmem() for _ in weights],
        out_specs=tile(D_MODEL),
        out_shape=jax.ShapeDtypeStruct((batch, seq, D_MODEL), _F32),
        scratch_shapes=[pltpu.VMEM((2, rows, LANES), _F32),
                        pltpu.VMEM((2, rows, LANES), _F32),
                        pltpu.VMEM((2 * LANES, ATTN_WIDTH), _BF16),
                        pltpu.VMEM((rows + CONV_SPAN, CONV_WIDTH), _F32),
                        pltpu.VMEM((SUBLANES, CONV_TILE + CONV_SPAN - SUBLANES, CONV_WIDTH), _F32),
                        pltpu.VMEM((rows, ATTN_WIDTH + CONV_WIDTH), _BF16)],
        compiler_params=pltpu.CompilerParams(dimension_semantics=("arbitrary",) * 2,
                                             vmem_limit_bytes=VMEM_LIMIT_BYTES),
    )(x1, *outs, *lses, glu, glu, glu, *weights)


def _ffn_weights(pre_g, w_gu, w_down, post_g):
    post_g = post_g * FFN_RESIDUAL_WEIGHT
    return (pre_g.reshape(1, D_MODEL), w_gu.astype(_BF16), w_down.astype(_BF16), post_g.reshape(1, D_MODEL))


def _layer(x, ffn1, mix_g, w_in_parts, conv_w, conv_b, ln_g, ln_b, w_out, mix_post_g, ffn2, final_g):
    batch, seq, _ = x.shape
    tokens = batch * seq
    qkv_views = [(d, ATTN_WIDTH, _BF16) for _ in range(3) for d in DILATIONS]
    x1, *qkv, glu = _ffn_call(
        _ffn_proj_kernel, "ffn_proj", FFN_TILE, x.reshape(tokens, D_MODEL), ffn1 + (mix_g,) + w_in_parts,
        [(1, D_MODEL, _F32)] + qkv_views + [(1, CONV_WIDTH, _F32)],
        extra_scratch=[pltpu.VMEM((LANE_CHUNKS, FFN_TILE, LANES), _F32)])
    per_batch = lambda a: a.reshape(batch, a.shape[0] // batch, a.shape[1])
    n = len(DILATIONS)
    branch_outs = [_attn_branch(per_batch(qkv[b]), per_batch(qkv[n + b]), per_batch(qkv[2 * n + b]), d)
                   for b, d in enumerate(DILATIONS)]
    x2 = _mixer_tail(per_batch(x1), branch_outs, per_batch(glu), conv_w, conv_b, ln_g, ln_b, w_out, mix_post_g)
    (y,) = _ffn_call(_ffn_final_kernel, "ffn_final", FFN_FINAL_TILE, x2.reshape(tokens, D_MODEL),
                     ffn2 + (final_g,), [(1, D_MODEL, _F32)])
    return y.reshape(batch, seq, D_MODEL)


def _layer_weights(l, ffn1_pre_g, ffn1_w_gu, ffn1_w_down, ffn1_post_g, mix_pre_g, w_in, conv_w, conv_b,
                   conv_ln_g, conv_ln_b, w_out, mix_post_g, ffn2_pre_g, ffn2_w_gu, ffn2_w_down, ffn2_post_g,
                   final_g):
    w_in_l = w_in[l].astype(_BF16)
    splits = (0, ATTN_WIDTH, 2 * ATTN_WIDTH, 3 * ATTN_WIDTH, 3 * ATTN_WIDTH + CONV_WIDTH,
              3 * ATTN_WIDTH + 2 * CONV_WIDTH)
    w_in_parts = tuple(w_in_l[:, a:b] for a, b in zip(splits[:-1], splits[1:]))
    row = lambda g, n: g[l].reshape(1, n)
    return (_ffn_weights(ffn1_pre_g[l], ffn1_w_gu[l], ffn1_w_down[l], ffn1_post_g[l]),
            row(mix_pre_g, D_MODEL), w_in_parts,
            conv_w[l].reshape(CONV_KERNEL, CONV_WIDTH), row(conv_b, CONV_WIDTH),
            row(conv_ln_g, CONV_WIDTH), row(conv_ln_b, CONV_WIDTH),
            w_out[l].astype(_BF16), row(mix_post_g, D_MODEL),
            _ffn_weights(ffn2_pre_g[l], ffn2_w_gu[l], ffn2_w_down[l], ffn2_post_g[l]),
            row(final_g, D_MODEL))


def kernel(x_prompt, x_sample, ffn1_pre_g, ffn1_w_gu, ffn1_w_down, ffn1_post_g, mix_pre_g, w_in, conv_w, conv_b, conv_ln_g, conv_ln_b, w_out, mix_post_g, ffn2_pre_g, ffn2_w_gu, ffn2_w_down, ffn2_post_g, final_g):
    params = (ffn1_pre_g, ffn1_w_gu, ffn1_w_down, ffn1_post_g, mix_pre_g, w_in, conv_w, conv_b, conv_ln_g,
              conv_ln_b, w_out, mix_post_g, ffn2_pre_g, ffn2_w_gu, ffn2_w_down, ffn2_post_g, final_g)
    layers = [_layer_weights(l, *params) for l in range(ffn1_pre_g.shape[0])]
    outs = []
    for x in (x_prompt, x_sample):
        for layer in layers:
            x = _layer(x, *layer)
        outs.append(x)
    return tuple(outs)
```

```python
import functools
import math

import jax
import jax.numpy as jnp
from jax import lax
from jax.experimental import pallas as pl
from jax.experimental.pallas import tpu as pltpu

D_MODEL = 1024
D_FF = 2816
ATTN_WIDTH = 512
CONV_WIDTH = 512
HEAD_DIM = 64
N_HEADS = ATTN_WIDTH // HEAD_DIM
HEAD_PAIRS = N_HEADS // 2
DILATED_BRANCHES = ((128, 1), (512, 4), (2048, 16))
DILATIONS = tuple(d for _, d in DILATED_BRANCHES)
HALF = 64
CONV_KERNEL = 31
CONV_PAD = CONV_KERNEL // 2
FFN_RESIDUAL_WEIGHT = 0.5
RMS_EPS = 1e-6
LN_EPS = 1e-5
NEG_INF = -1e30
LOG2_E = math.log2(math.e)
LN_2 = math.log(2.0)

LANES = 128
SUBLANES = 8
LANE_CHUNKS = ATTN_WIDTH // LANES
HEAD_LANES = LANES // N_HEADS
Q_SUB = 2 * HALF
K_WIN = Q_SUB + 2 * HALF
FF_CHUNK = 256
N_FF_CHUNKS = D_FF // FF_CHUNK
ATTN_STEP_POSITIONS = 2048
FFN_TILE = 512
FFN_FINAL_TILE = 1024
TAIL_TILE = 512
CONV_HALO = 16
CONV_TILE = 512
CONV_ROWS = 256
CONV_FIRST = CONV_HALO - CONV_PAD
CONV_SPAN = -(-(CONV_FIRST + CONV_KERNEL - 1) // SUBLANES) * SUBLANES
VMEM_LIMIT_BYTES = 56 * 1024 * 1024

assert all(w // (2 * d) == HALF for w, d in DILATED_BRANCHES)
assert DILATIONS[0] == 1 and D_FF % FF_CHUNK == 0
assert math.frexp(FFN_RESIDUAL_WEIGHT)[0] == 0.5

_BF16 = jnp.bfloat16
_F32 = jnp.float32


def _rms(x, g):
    ms = jnp.mean(x * x, axis=-1, keepdims=True)
    return x * lax.rsqrt(ms + RMS_EPS) * g


def _whole_vmem():
    return pl.BlockSpec(memory_space=pltpu.VMEM)


def _swiglu(h_scr, a_scr, wgu, wd):
    for c in range(N_FF_CHUNKS):
        h = h_scr[...]
        gate = jnp.dot(h, wgu[:, c * FF_CHUNK:(c + 1) * FF_CHUNK], preferred_element_type=_F32)
        up = jnp.dot(h, wgu[:, D_FF + c * FF_CHUNK:D_FF + (c + 1) * FF_CHUNK], preferred_element_type=_F32)
        a_scr[:, c * FF_CHUNK:(c + 1) * FF_CHUNK] = (gate * jax.nn.sigmoid(gate) * up).astype(_BF16)
    return jnp.dot(a_scr[...], wd[...], preferred_element_type=_F32)


def _store_views(proj, view_refs, lane_scr):
    rows = proj.shape[0]
    view_refs[0][...] = proj.astype(_BF16)
    for c in range(LANE_CHUNKS):
        lane_scr[c] = proj[:, c * LANES:(c + 1) * LANES]
    for dilation, ref in zip(DILATIONS[1:], view_refs[1:]):
        for r in range(dilation):
            for c in range(LANE_CHUNKS):
                col = r * ATTN_WIDTH + c * LANES
                ref[:, col:col + LANES] = lane_scr[c, pl.ds(r, rows // dilation, stride=dilation), :].astype(_BF16)


def _ffn_proj_kernel(x_ref, pre_g, wgu, wd, post_g, mix_g, wq, wk, wv, wa, wb,
                     x1_ref, q1_ref, q4_ref, q16_ref, k1_ref, k4_ref, k16_ref, v1_ref, v4_ref, v16_ref, glu_ref,
                     h_scr, a_scr, lane_scr):
    h_scr[...] = _rms(x_ref[...], pre_g[...]).astype(_BF16)
    x1 = x_ref[...] + _rms(_swiglu(h_scr, a_scr, wgu, wd), post_g[...])
    x1_ref[...] = x1
    h_scr[...] = _rms(x1, mix_g[...]).astype(_BF16)
    q = jnp.dot(h_scr[...], wq[...], preferred_element_type=_F32) * (HEAD_DIM ** -0.5 * LOG2_E)
    _store_views(q, (q1_ref, q4_ref, q16_ref), lane_scr)
    k = jnp.dot(h_scr[...], wk[...], preferred_element_type=_F32)
    _store_views(k, (k1_ref, k4_ref, k16_ref), lane_scr)
    v = jnp.dot(h_scr[...], wv[...], preferred_element_type=_F32)
    _store_views(v, (v1_ref, v4_ref, v16_ref), lane_scr)
    ga = jnp.dot(h_scr[...], wa[...], preferred_element_type=_F32)
    gb = jnp.dot(h_scr[...], wb[...], preferred_element_type=_F32)
    glu_ref[...] = ga * jax.nn.sigmoid(gb)


def _ffn_final_kernel(x_ref, pre_g, wgu, wd, post_g, final_g, y_ref, h_scr, a_scr):
    for blk in range(FFN_FINAL_TILE // FFN_TILE):
        rows = slice(blk * FFN_TILE, (blk + 1) * FFN_TILE)
        h_blk, a_blk = h_scr.at[rows], a_scr.at[rows]
        h_blk[...] = _rms(x_ref[rows, :], pre_g[...]).astype(_BF16)
        x3 = x_ref[rows, :] + _rms(_swiglu(h_blk, a_blk, wgu, wd), post_g[...])
        y_ref[rows, :] = _rms(x3, final_g[...])


def _ffn_call(body, name, tm, x, weights, outs, extra_scratch=()):
    n_tokens = x.shape[0]
    row_spec = lambda d, w: pl.BlockSpec((tm // d, d * w), lambda i: (i, 0))
    return pl.pallas_call(
        body,
        name=name,
        grid=(n_tokens // tm,),
        in_specs=[row_spec(1, D_MODEL)] + [_whole_vmem() for _ in weights],
        out_specs=[row_spec(d, w) for d, w, _ in outs],
        out_shape=[jax.ShapeDtypeStruct((n_tokens // d, d * w), dt) for d, w, dt in outs],
        scratch_shapes=[pltpu.VMEM((tm, D_MODEL), _BF16), pltpu.VMEM((tm, D_FF), _BF16), *extra_scratch],
        compiler_params=pltpu.CompilerParams(dimension_semantics=("arbitrary",),
                                             vmem_limit_bytes=VMEM_LIMIT_BYTES),
    )(x, *weights)


def _attn_kernel(q_ref, kp_ref, k_ref, kn_ref, vp_ref, v_ref, vn_ref, o_ref, lse_ref, bias_scr,
                 *, dilation, rows, residues):
    j = pl.program_id(2)
    n_sub = rows // Q_SUB

    @pl.when((pl.program_id(0) == 0) & (pl.program_id(1) == 0) & (j == 0))
    def _():
        row = lax.broadcasted_iota(jnp.int32, (2 * Q_SUB, K_WIN), 0)
        col = lax.broadcasted_iota(jnp.int32, (2 * Q_SUB, K_WIN), 1)
        rel = jnp.abs(col - HALF - (row & (Q_SUB - 1)))
        dist = rel.astype(_F32)
        for p in range(HEAD_PAIRS):
            slope_even = dilation * 2.0 ** (-8.0 * (2 * p + 1) / N_HEADS)
            slope_odd = dilation * 2.0 ** (-8.0 * (2 * p + 2) / N_HEADS)
            slope = jnp.where(row < Q_SUB, slope_even, slope_odd)
            bias_scr[p] = jnp.where(rel <= HALF, -slope * LOG2_E * dist, NEG_INF)

    lane = lax.broadcasted_iota(jnp.int32, (Q_SUB, LANES), 1)
    even_lanes = lane < HEAD_DIM
    col = lax.broadcasted_iota(jnp.int32, (2 * Q_SUB, K_WIN), 1)
    first_valid_col = jnp.where(j == 0, HALF, 0)
    last_valid_col = jnp.where(j == pl.num_programs(2) - 1, K_WIN - HALF, K_WIN)
    ones = jnp.ones((K_WIN, LANES), _BF16)

    def window(prev_ref, main_ref, next_ref, i, lanes):
        parts = []
        lo = i * Q_SUB - HALF
        hi = lo + K_WIN
        if lo < 0:
            parts.append(prev_ref[0, :, lanes])
            lo = 0
        parts.append(main_ref[0, lo:min(hi, rows), lanes])
        if hi > rows:
            parts.append(next_ref[0, :, lanes])
        return parts[0] if len(parts) == 1 else jnp.concatenate(parts, axis=0)

    pair_lane = (lax.broadcasted_iota(jnp.int32, (Q_SUB, LANES), 1) % HEAD_DIM) // HEAD_LANES
    for res, i in ((res, i) for res in range(residues) for i in range(n_sub)):
        q_rows = slice(i * Q_SUB, (i + 1) * Q_SUB)
        packed = None
        for p in range(HEAD_PAIRS):
            lanes = slice(res * ATTN_WIDTH + p * LANES, res * ATTN_WIDTH + (p + 1) * LANES)
            q2 = q_ref[0, q_rows, lanes]
            zero = jnp.zeros_like(q2)
            q_stack = jnp.concatenate([jnp.where(even_lanes, q2, zero), jnp.where(even_lanes, zero, q2)], axis=0)
            kw = window(kp_ref, k_ref, kn_ref, i, lanes)
            s = lax.dot_general(q_stack, kw, (((1,), (1,)), ((), ())), preferred_element_type=_F32)
            s = s + bias_scr[p]
            if i == 0:
                s = jnp.where(col < first_valid_col, NEG_INF, s)
            if i == n_sub - 1:
                s = jnp.where(col >= last_valid_col, NEG_INF, s)
            m = jnp.max(s, axis=-1, keepdims=True)
            e = jnp.exp2(s - m).astype(_BF16)
            vw = window(vp_ref, v_ref, vn_ref, i, lanes)
            r = jnp.dot(e, jnp.concatenate([vw, ones], axis=1), preferred_element_type=_F32)
            acc = jnp.where(even_lanes, r[:Q_SUB, :LANES], r[Q_SUB:, :LANES])
            den = jnp.where(even_lanes, r[:Q_SUB, LANES:], r[Q_SUB:, LANES:])
            mx = jnp.where(even_lanes, m[:Q_SUB], m[Q_SUB:])
            o_ref[0, q_rows, lanes] = acc / den
            lse = mx * LN_2 + jnp.log(den)
            packed = lse if packed is None else jnp.where(pair_lane == p, lse, packed)
        lse_ref[0, q_rows, res * LANES:(res + 1) * LANES] = packed


def _attn_branch(q, k, v, dilation):
    batch, sub_len, _ = q.shape
    rows = min(sub_len, ATTN_STEP_POSITIONS)
    residues = min(dilation, ATTN_STEP_POSITIONS // rows)
    width = residues * ATTN_WIDTH
    halo_per_tile = rows // HALF
    last_halo = sub_len // HALF - 1
    main = pl.BlockSpec((1, rows, width), lambda b, r, j: (b, j, r))
    prev = pl.BlockSpec((1, HALF, width), lambda b, r, j: (b, jnp.maximum(j * halo_per_tile - 1, 0), r))
    nxt = pl.BlockSpec((1, HALF, width), lambda b, r, j: (b, jnp.minimum((j + 1) * halo_per_tile, last_halo), r))
    lse_block = pl.BlockSpec((1, rows, residues * LANES), lambda b, r, j: (b, j, r))
    out_shapes = [jax.ShapeDtypeStruct((batch, sub_len, dilation * w), _F32) for w in (ATTN_WIDTH, LANES)]
    return pl.pallas_call(
        functools.partial(_attn_kernel, dilation=dilation, rows=rows, residues=residues),
        name=f"attn_dilation{dilation}",
        grid=(batch, dilation // residues, sub_len // rows),
        in_specs=[main, prev, main, nxt, prev, main, nxt],
        out_specs=[main, lse_block],
        out_shape=out_shapes,
        scratch_shapes=[pltpu.VMEM((HEAD_PAIRS, 2 * Q_SUB, K_WIN), _F32)],
        compiler_params=pltpu.CompilerParams(dimension_semantics=("arbitrary",) * 3,
                                             vmem_limit_bytes=VMEM_LIMIT_BYTES),
    )(q, k, k, k, v, v, v)


def _mixer_tail_kernel(x1_ref, o1, o4, o16, l1, l4, l16, gp_ref, g_ref, gn_ref,
                       conv_w, conv_b, ln_g, ln_b, w_out, post_g, x2_ref,
                       nat_scr, lse_scr, expand_scr, g_scr, shift_scr, m_scr, *, rows):
    i = pl.program_id(1)

    @pl.when((pl.program_id(0) == 0) & (i == 0))
    def _():
        src = lax.broadcasted_iota(jnp.int32, (2 * LANES, ATTN_WIDTH), 0) % LANES
        head = lax.broadcasted_iota(jnp.int32, (2 * LANES, ATTN_WIDTH), 1) // HEAD_DIM
        packed_lane = (head % 2) * HEAD_DIM + (head // 2) * HEAD_LANES
        expand_scr[...] = jnp.where(src == packed_lane, 1.0, 0.0).astype(_BF16)

    for a, ref in enumerate((l4, l16)):
        dilation = DILATIONS[1 + a]
        for r in range(dilation):
            lse_scr[a, pl.ds(r, rows // dilation, stride=dilation), :] = ref[0, :, r * LANES:(r + 1) * LANES]
    lses = (l1[0], lse_scr[0], lse_scr[1])
    top = jnp.maximum(jnp.maximum(lses[0], lses[1]), lses[2])
    w = [jnp.exp(l - top) for l in lses]
    total = w[0] + w[1] + w[2]
    weights = []
    for wb in w:
        wn = wb / total
        hi = wn.astype(_BF16)
        lo = (wn - hi.astype(_F32)).astype(_BF16)
        weights.append(jnp.dot(jnp.concatenate([hi, lo], axis=1), expand_scr[...], preferred_element_type=_F32))

    for c in range(LANE_CHUNKS):
        lanes = slice(c * LANES, (c + 1) * LANES)
        for a, ref in enumerate((o4, o16)):
            dilation = DILATIONS[1 + a]
            for r in range(dilation):
                col = r * ATTN_WIDTH + c * LANES
                nat_scr[a, pl.ds(r, rows // dilation, stride=dilation), :] = ref[0, :, col:col + LANES]
        attn = (weights[0][:, lanes] * o1[0, :, lanes] + weights[1][:, lanes] * nat_scr[0]
                + weights[2][:, lanes] * nat_scr[1])
        m_scr[:, lanes] = attn.astype(_BF16)

    g_scr[:CONV_HALO] = jnp.where(i > 0, gp_ref[0], 0.0)
    g_scr[CONV_HALO:CONV_HALO + rows] = g_ref[0]
    g_scr[CONV_HALO + rows:CONV_HALO + rows + CONV_HALO] = jnp.where(i < pl.num_programs(1) - 1, gn_ref[0], 0.0)
    for t0 in range(0, rows, CONV_TILE):
        for b in range(SUBLANES):
            shift_scr[b] = g_scr[t0 + b:t0 + b + CONV_TILE + CONV_SPAN - SUBLANES, :]
        for base in range(0, CONV_TILE, CONV_ROWS):
            acc = jnp.broadcast_to(conv_b[...], (CONV_ROWS, CONV_WIDTH))
            for t in range(CONV_KERNEL):
                off = CONV_FIRST + t
                start = base + off // SUBLANES * SUBLANES
                acc = acc + conv_w[t:t + 1, :] * shift_scr[off % SUBLANES, start:start + CONV_ROWS, :]
            mu = jnp.mean(acc, axis=-1, keepdims=True)
            cen = acc - mu
            var = jnp.mean(cen * cen, axis=-1, keepdims=True)
            y = cen * lax.rsqrt(var + LN_EPS) * ln_g[...] + ln_b[...]
            m_scr[t0 + base:t0 + base + CONV_ROWS, ATTN_WIDTH:] = (y * jax.nn.sigmoid(y)).astype(_BF16)

    mixed = jnp.dot(m_scr[...], w_out[...], preferred_element_type=_F32)
    x2_ref[0] = x1_ref[0] + _rms(mixed, post_g[...])


def _mixer_tail(x1, branch_outs, glu, conv_w, conv_b, ln_g, ln_b, w_out, post_g):
    batch, seq, _ = x1.shape
    rows = TAIL_TILE
    halo_per_tile = rows // CONV_HALO
    last_halo = seq // CONV_HALO - 1
    tile = lambda w, d=1: pl.BlockSpec((1, rows // d, d * w), lambda b, i: (b, i, 0))
    prev = pl.BlockSpec((1, CONV_HALO, CONV_WIDTH), lambda b, i: (b, jnp.maximum(i * halo_per_tile - 1, 0), 0))
    nxt = pl.BlockSpec((1, CONV_HALO, CONV_WIDTH),
                       lambda b, i: (b, jnp.minimum((i + 1) * halo_per_tile, last_halo), 0))
    outs = [o for o, _ in branch_outs]
    lses = [l for _, l in branch_outs]
    views = lambda w: [tile(w, d) for d in DILATIONS]
    weights = (conv_w, conv_b, ln_g, ln_b, w_out, post_g)
    return pl.pallas_call(
        functools.partial(_mixer_tail_kernel, rows=rows),
        name="mixer_tail",
        grid=(batch, seq // rows),
        in_specs=[tile(D_MODEL)] + views(ATTN_WIDTH) + views(LANES) + [prev, tile(CONV_WIDTH), nxt]
                 + [_whole_vmem() for _ in weights],
        out_specs=tile(D_MODEL),
        out_shape=jax.ShapeDtypeStruct((batch, seq, D_MODEL), _F32),
        scratch_shapes=[pltpu.VMEM((2, rows, LANES), _F32),
                        pltpu.VMEM((2, rows, LANES), _F32),
                        pltpu.VMEM((2 * LANES, ATTN_WIDTH), _BF16),
                        pltpu.VMEM((rows + CONV_SPAN, CONV_WIDTH), _F32),
                        pltpu.VMEM((SUBLANES, CONV_TILE + CONV_SPAN - SUBLANES, CONV_WIDTH), _F32),
                        pltpu.VMEM((rows, ATTN_WIDTH + CONV_WIDTH), _BF16)],
        compiler_params=pltpu.CompilerParams(dimension_semantics=("arbitrary",) * 2,
                                             vmem_limit_bytes=VMEM_LIMIT_BYTES),
    )(x1, *outs, *lses, glu, glu, glu, *weights)


def _ffn_weights(pre_g, w_gu, w_down, post_g):
    post_g = post_g * FFN_RESIDUAL_WEIGHT
    return (pre_g.reshape(1, D_MODEL), w_gu.astype(_BF16), w_down.astype(_BF16), post_g.reshape(1, D_MODEL))


def _layer(x, ffn1, mix_g, w_in_parts, conv_w, conv_b, ln_g, ln_b, w_out, mix_post_g, ffn2, final_g):
    batch, seq, _ = x.shape
    tokens = batch * seq
    qkv_views = [(d, ATTN_WIDTH, _BF16) for _ in range(3) for d in DILATIONS]
    x1, *qkv, glu = _ffn_call(
        _ffn_proj_kernel, "ffn_proj", FFN_TILE, x.reshape(tokens, D_MODEL), ffn1 + (mix_g,) + w_in_parts,
        [(1, D_MODEL, _F32)] + qkv_views + [(1, CONV_WIDTH, _F32)],
        extra_scratch=[pltpu.VMEM((LANE_CHUNKS, FFN_TILE, LANES), _F32)])
    per_batch = lambda a: a.reshape(batch, a.shape[0] // batch, a.shape[1])
    n = len(DILATIONS)
    branch_outs = [_attn_branch(per_batch(qkv[b]), per_batch(qkv[n + b]), per_batch(qkv[2 * n + b]), d)
                   for b, d in enumerate(DILATIONS)]
    x2 = _mixer_tail(per_batch(x1), branch_outs, per_batch(glu), conv_w, conv_b, ln_g, ln_b, w_out, mix_post_g)
    (y,) = _ffn_call(_ffn_final_kernel, "ffn_final", FFN_FINAL_TILE, x2.reshape(tokens, D_MODEL),
                     ffn2 + (final_g,), [(1, D_MODEL, _F32)])
    return y.reshape(batch, seq, D_MODEL)


def _layer_weights(l, ffn1_pre_g, ffn1_w_gu, ffn1_w_down, ffn1_post_g, mix_pre_g, w_in, conv_w, conv_b,
                   conv_ln_g, conv_ln_b, w_out, mix_post_g, ffn2_pre_g, ffn2_w_gu, ffn2_w_down, ffn2_post_g,
                   final_g):
    w_in_l = w_in[l].astype(_BF16)
    splits = (0, ATTN_WIDTH, 2 * ATTN_WIDTH, 3 * ATTN_WIDTH, 3 * ATTN_WIDTH + CONV_WIDTH,
              3 * ATTN_WIDTH + 2 * CONV_WIDTH)
    w_in_parts = tuple(w_in_l[:, a:b] for a, b in zip(splits[:-1], splits[1:]))
    row = lambda g, n: g[l].reshape(1, n)
    return (_ffn_weights(ffn1_pre_g[l], ffn1_w_gu[l], ffn1_w_down[l], ffn1_post_g[l]),
            row(mix_pre_g, D_MODEL), w_in_parts,
            conv_w[l].reshape(CONV_KERNEL, CONV_WIDTH), row(conv_b, CONV_WIDTH),
            row(conv_ln_g, CONV_WIDTH), row(conv_ln_b, CONV_WIDTH),
            w_out[l].astype(_BF16), row(mix_post_g, D_MODEL),
            _ffn_weights(ffn2_pre_g[l], ffn2_w_gu[l], ffn2_w_down[l], ffn2_post_g[l]),
            row(final_g, D_MODEL))


def kernel(x_prompt, x_sample, ffn1_pre_g, ffn1_w_gu, ffn1_w_down, ffn1_post_g, mix_pre_g, w_in, conv_w, conv_b, conv_ln_g, conv_ln_b, w_out, mix_post_g, ffn2_pre_g, ffn2_w_gu, ffn2_w_down, ffn2_post_g, final_g):
    params = (ffn1_pre_g, ffn1_w_gu, ffn1_w_down, ffn1_post_g, mix_pre_g, w_in, conv_w, conv_b, conv_ln_g,
              conv_ln_b, w_out, mix_post_g, ffn2_pre_g, ffn2_w_gu, ffn2_w_down, ffn2_post_g, final_g)
    layers = [_layer_weights(l, *params) for l in range(ffn1_pre_g.shape[0])]
    outs = []
    for x in (x_prompt, x_sample):
        for layer in layers:
            x = _layer(x, *layer)
        outs.append(x)
    return tuple(outs)
```

```python
import functools
import math

import jax
import jax.numpy as jnp
from jax import lax
from jax.experimental import pallas as pl
from jax.experimental.pallas import tpu as pltpu

D_MODEL = 1024
D_FF = 2816
ATTN_WIDTH = 512
CONV_WIDTH = 512
HEAD_DIM = 64
N_HEADS = ATTN_WIDTH // HEAD_DIM
HEAD_PAIRS = N_HEADS // 2
DILATED_BRANCHES = ((128, 1), (512, 4), (2048, 16))
DILATIONS = tuple(d for _, d in DILATED_BRANCHES)
HALF = 64
CONV_KERNEL = 31
CONV_PAD = CONV_KERNEL // 2
FFN_RESIDUAL_WEIGHT = 0.5
RMS_EPS = 1e-6
LN_EPS = 1e-5
NEG_INF = -1e30
LOG2_E = math.log2(math.e)
LN_2 = math.log(2.0)

LANES = 128
SUBLANES = 8
LANE_CHUNKS = ATTN_WIDTH // LANES
HEAD_LANES = LANES // N_HEADS
Q_SUB = 2 * HALF
K_WIN = Q_SUB + 2 * HALF
FF_CHUNK = 256
N_FF_CHUNKS = D_FF // FF_CHUNK
ATTN_STEP_POSITIONS = 2048
FFN_TILE = 512
FFN_FINAL_TILE = 1024
TAIL_TILE = 512
CONV_HALO = 16
CONV_TILE = 512
CONV_ROWS = 256
CONV_FIRST = CONV_HALO - CONV_PAD
CONV_SPAN = -(-(CONV_FIRST + CONV_KERNEL - 1) // SUBLANES) * SUBLANES
VMEM_LIMIT_BYTES = 56 * 1024 * 1024

assert all(w // (2 * d) == HALF for w, d in DILATED_BRANCHES)
assert DILATIONS[0] == 1 and D_FF % FF_CHUNK == 0
assert math.frexp(FFN_RESIDUAL_WEIGHT)[0] == 0.5

_BF16 = jnp.bfloat16
_F32 = jnp.float32


def _rms(x, g):
    ms = jnp.mean(x * x, axis=-1, keepdims=True)
    return x * lax.rsqrt(ms + RMS_EPS) * g


def _whole_vmem():
    return pl.BlockSpec(memory_space=pltpu.VMEM)


def _swiglu(h_scr, a_scr, wgu, wd):
    for c in range(N_FF_CHUNKS):
        h = h_scr[...]
        gate = jnp.dot(h, wgu[:, c * FF_CHUNK:(c + 1) * FF_CHUNK], preferred_element_type=_F32)
        up = jnp.dot(h, wgu[:, D_FF + c * FF_CHUNK:D_FF + (c + 1) * FF_CHUNK], preferred_element_type=_F32)
        a_scr[:, c * FF_CHUNK:(c + 1) * FF_CHUNK] = (gate * jax.nn.sigmoid(gate) * up).astype(_BF16)
    return jnp.dot(a_scr[...], wd[...], preferred_element_type=_F32)


def _store_views(proj, view_refs, lane_scr):
    rows = proj.shape[0]
    view_refs[0][...] = proj.astype(_BF16)
    for c in range(LANE_CHUNKS):
        lane_scr[c] = proj[:, c * LANES:(c + 1) * LANES]
    for dilation, ref in zip(DILATIONS[1:], view_refs[1:]):
        for r in range(dilation):
            for c in range(LANE_CHUNKS):
                col = r * ATTN_WIDTH + c * LANES
                ref[:, col:col + LANES] = lane_scr[c, pl.ds(r, rows // dilation, stride=dilation), :].astype(_BF16)


def _ffn_proj_kernel(x_ref, pre_g, wgu, wd, post_g, mix_g, wq, wk, wv, wa, wb,
                     x1_ref, q1_ref, q4_ref, q16_ref, k1_ref, k4_ref, k16_ref, v1_ref, v4_ref, v16_ref, glu_ref,
                     h_scr, a_scr, lane_scr):
    h_scr[...] = _rms(x_ref[...], pre_g[...]).astype(_BF16)
    x1 = x_ref[...] + _rms(_swiglu(h_scr, a_scr, wgu, wd), post_g[...])
    x1_ref[...] = x1
    h_scr[...] = _rms(x1, mix_g[...]).astype(_BF16)
    q = jnp.dot(h_scr[...], wq[...], preferred_element_type=_F32) * (HEAD_DIM ** -0.5 * LOG2_E)
    _store_views(q, (q1_ref, q4_ref, q16_ref), lane_scr)
    k = jnp.dot(h_scr[...], wk[...], preferred_element_type=_F32)
    _store_views(k, (k1_ref, k4_ref, k16_ref), lane_scr)
    v = jnp.dot(h_scr[...], wv[...], preferred_element_type=_F32)
    _store_views(v, (v1_ref, v4_ref, v16_ref), lane_scr)
    ga = jnp.dot(h_scr[...], wa[...], preferred_element_type=_F32)
    gb = jnp.dot(h_scr[...], wb[...], preferred_element_type=_F32)
    glu_ref[...] = ga * jax.nn.sigmoid(gb)


def _ffn_final_kernel(x_ref, pre_g, wgu, wd, post_g, final_g, y_ref, h_scr, a_scr):
    h_scr[...] = _rms(x_ref[...], pre_g[...]).astype(_BF16)
    x3 = x_ref[...] + _rms(_swiglu(h_scr, a_scr, wgu, wd), post_g[...])
    y_ref[...] = _rms(x3, final_g[...])


def _ffn_call(body, name, tm, x, weights, outs, extra_scratch=()):
    n_tokens = x.shape[0]
    row_spec = lambda d, w: pl.BlockSpec((tm // d, d * w), lambda i: (i, 0))
    return pl.pallas_call(
        body,
        name=name,
        grid=(n_tokens // tm,),
        in_specs=[row_spec(1, D_MODEL)] + [_whole_vmem() for _ in weights],
        out_specs=[row_spec(d, w) for d, w, _ in outs],
        out_shape=[jax.ShapeDtypeStruct((n_tokens // d, d * w), dt) for d, w, dt in outs],
        scratch_shapes=[pltpu.VMEM((tm, D_MODEL), _BF16), pltpu.VMEM((tm, D_FF), _BF16), *extra_scratch],
        compiler_params=pltpu.CompilerParams(dimension_semantics=("arbitrary",),
                                             vmem_limit_bytes=VMEM_LIMIT_BYTES),
    )(x, *weights)


def _attn_kernel(q_ref, kp_ref, k_ref, kn_ref, vp_ref, v_ref, vn_ref, o_ref, lse_ref, bias_scr,
                 *, dilation, rows, residues):
    j = pl.program_id(2)
    n_sub = rows // Q_SUB

    @pl.when((pl.program_id(0) == 0) & (pl.program_id(1) == 0) & (j == 0))
    def _():
        row = lax.broadcasted_iota(jnp.int32, (2 * Q_SUB, K_WIN), 0)
        col = lax.broadcasted_iota(jnp.int32, (2 * Q_SUB, K_WIN), 1)
        rel = jnp.abs(col - HALF - (row & (Q_SUB - 1)))
        dist = rel.astype(_F32)
        for p in range(HEAD_PAIRS):
            slope_even = dilation * 2.0 ** (-8.0 * (2 * p + 1) / N_HEADS)
            slope_odd = dilation * 2.0 ** (-8.0 * (2 * p + 2) / N_HEADS)
            slope = jnp.where(row < Q_SUB, slope_even, slope_odd)
            bias_scr[p] = jnp.where(rel <= HALF, -slope * LOG2_E * dist, NEG_INF)

    lane = lax.broadcasted_iota(jnp.int32, (Q_SUB, LANES), 1)
    even_lanes = lane < HEAD_DIM
    col = lax.broadcasted_iota(jnp.int32, (2 * Q_SUB, K_WIN), 1)
    first_valid_col = jnp.where(j == 0, HALF, 0)
    last_valid_col = jnp.where(j == pl.num_programs(2) - 1, K_WIN - HALF, K_WIN)
    ones = jnp.ones((K_WIN, LANES), _BF16)

    def window(prev_ref, main_ref, next_ref, i, lanes):
        parts = []
        lo = i * Q_SUB - HALF
        hi = lo + K_WIN
        if lo < 0:
            parts.append(prev_ref[0, :, lanes])
            lo = 0
        parts.append(main_ref[0, lo:min(hi, rows), lanes])
        if hi > rows:
            parts.append(next_ref[0, :, lanes])
        return parts[0] if len(parts) == 1 else jnp.concatenate(parts, axis=0)

    pair_lane = (lax.broadcasted_iota(jnp.int32, (Q_SUB, LANES), 1) % HEAD_DIM) // HEAD_LANES
    for res, i in ((res, i) for res in range(residues) for i in range(n_sub)):
        q_rows = slice(i * Q_SUB, (i + 1) * Q_SUB)
        packed = None
        for p in range(HEAD_PAIRS):
            lanes = slice(res * ATTN_WIDTH + p * LANES, res * ATTN_WIDTH + (p + 1) * LANES)
            q2 = q_ref[0, q_rows, lanes]
            zero = jnp.zeros_like(q2)
            q_stack = jnp.concatenate([jnp.where(even_lanes, q2, zero), jnp.where(even_lanes, zero, q2)], axis=0)
            kw = window(kp_ref, k_ref, kn_ref, i, lanes)
            s = lax.dot_general(q_stack, kw, (((1,), (1,)), ((), ())), preferred_element_type=_F32)
            s = s + bias_scr[p]
            if i == 0:
                s = jnp.where(col < first_valid_col, NEG_INF, s)
            if i == n_sub - 1:
                s = jnp.where(col >= last_valid_col, NEG_INF, s)
            m = jnp.max(s, axis=-1, keepdims=True)
            e = jnp.exp2(s - m).astype(_BF16)
            vw = window(vp_ref, v_ref, vn_ref, i, lanes)
            r = jnp.dot(e, jnp.concatenate([vw, ones], axis=1), preferred_element_type=_F32)
            acc = jnp.where(even_lanes, r[:Q_SUB, :LANES], r[Q_SUB:, :LANES])
            den = jnp.where(even_lanes, r[:Q_SUB, LANES:], r[Q_SUB:, LANES:])
            mx = jnp.where(even_lanes, m[:Q_SUB], m[Q_SUB:])
            o_ref[0, q_rows, lanes] = acc / den
            lse = mx * LN_2 + jnp.log(den)
            packed = lse if packed is None else jnp.where(pair_lane == p, lse, packed)
        lse_ref[0, q_rows, res * LANES:(res + 1) * LANES] = packed


def _attn_branch(q, k, v, dilation):
    batch, sub_len, _ = q.shape
    rows = min(sub_len, ATTN_STEP_POSITIONS)
    residues = min(dilation, ATTN_STEP_POSITIONS // rows)
    width = residues * ATTN_WIDTH
    halo_per_tile = rows // HALF
    last_halo = sub_len // HALF - 1
    main = pl.BlockSpec((1, rows, width), lambda b, r, j: (b, j, r))
    prev = pl.BlockSpec((1, HALF, width), lambda b, r, j: (b, jnp.maximum(j * halo_per_tile - 1, 0), r))
    nxt = pl.BlockSpec((1, HALF, width), lambda b, r, j: (b, jnp.minimum((j + 1) * halo_per_tile, last_halo), r))
    lse_block = pl.BlockSpec((1, rows, residues * LANES), lambda b, r, j: (b, j, r))
    out_shapes = [jax.ShapeDtypeStruct((batch, sub_len, dilation * w), _F32) for w in (ATTN_WIDTH, LANES)]
    return pl.pallas_call(
        functools.partial(_attn_kernel, dilation=dilation, rows=rows, residues=residues),
        name=f"attn_dilation{dilation}",
        grid=(batch, dilation // residues, sub_len // rows),
        in_specs=[main, prev, main, nxt, prev, main, nxt],
        out_specs=[main, lse_block],
        out_shape=out_shapes,
        scratch_shapes=[pltpu.VMEM((HEAD_PAIRS, 2 * Q_SUB, K_WIN), _F32)],
        compiler_params=pltpu.CompilerParams(dimension_semantics=("arbitrary",) * 3,
                                             vmem_limit_bytes=VMEM_LIMIT_BYTES),
    )(q, k, k, k, v, v, v)


def _mixer_tail_kernel(x1_ref, o1, o4, o16, l1, l4, l16, gp_ref, g_ref, gn_ref,
                       conv_w, conv_b, ln_g, ln_b, w_out, post_g, x2_ref,
                       nat_scr, lse_scr, expand_scr, g_scr, shift_scr, m_scr, *, rows):
    i = pl.program_id(1)

    @pl.when((pl.program_id(0) == 0) & (i == 0))
    def _():
        src = lax.broadcasted_iota(jnp.int32, (2 * LANES, ATTN_WIDTH), 0) % LANES
        head = lax.broadcasted_iota(jnp.int32, (2 * LANES, ATTN_WIDTH), 1) // HEAD_DIM
        packed_lane = (head % 2) * HEAD_DIM + (head // 2) * HEAD_LANES
        expand_scr[...] = jnp.where(src == packed_lane, 1.0, 0.0).astype(_BF16)

    for a, ref in enumerate((l4, l16)):
        dilation = DILATIONS[1 + a]
        for r in range(dilation):
            lse_scr[a, pl.ds(r, rows // dilation, stride=dilation), :] = ref[0, :, r * LANES:(r + 1) * LANES]
    lses = (l1[0], lse_scr[0], lse_scr[1])
    top = jnp.maximum(jnp.maximum(lses[0], lses[1]), lses[2])
    w = [jnp.exp(l - top) for l in lses]
    total = w[0] + w[1] + w[2]
    weights = []
    for wb in w:
        wn = wb / total
        hi = wn.astype(_BF16)
        lo = (wn - hi.astype(_F32)).astype(_BF16)
        weights.append(jnp.dot(jnp.concatenate([hi, lo], axis=1), expand_scr[...], preferred_element_type=_F32))

    for c in range(LANE_CHUNKS):
        lanes = slice(c * LANES, (c + 1) * LANES)
        for a, ref in enumerate((o4, o16)):
            dilation = DILATIONS[1 + a]
            for r in range(dilation):
                col = r * ATTN_WIDTH + c * LANES
                nat_scr[a, pl.ds(r, rows // dilation, stride=dilation), :] = ref[0, :, col:col + LANES]
        attn = (weights[0][:, lanes] * o1[0, :, lanes] + weights[1][:, lanes] * nat_scr[0]
                + weights[2][:, lanes] * nat_scr[1])
        m_scr[:, lanes] = attn.astype(_BF16)

    g_scr[:CONV_HALO] = jnp.where(i > 0, gp_ref[0], 0.0)
    g_scr[CONV_HALO:CONV_HALO + rows] = g_ref[0]
    g_scr[CONV_HALO + rows:CONV_HALO + rows + CONV_HALO] = jnp.where(i < pl.num_programs(1) - 1, gn_ref[0], 0.0)
    for t0 in range(0, rows, CONV_TILE):
        for b in range(SUBLANES):
            shift_scr[b] = g_scr[t0 + b:t0 + b + CONV_TILE + CONV_SPAN - SUBLANES, :]
        for base in range(0, CONV_TILE, CONV_ROWS):
            acc = jnp.broadcast_to(conv_b[...], (CONV_ROWS, CONV_WIDTH))
            for t in range(CONV_KERNEL):
                off = CONV_FIRST + t
                start = base + off // SUBLANES * SUBLANES
                acc = acc + conv_w[t:t + 1, :] * shift_scr[off % SUBLANES, start:start + CONV_ROWS, :]
            mu = jnp.mean(acc, axis=-1, keepdims=True)
            cen = acc - mu
            var = jnp.mean(cen * cen, axis=-1, keepdims=True)
            y = cen * lax.rsqrt(var + LN_EPS) * ln_g[...] + ln_b[...]
            m_scr[t0 + base:t0 + base + CONV_ROWS, ATTN_WIDTH:] = (y * jax.nn.sigmoid(y)).astype(_BF16)

    mixed = jnp.dot(m_scr[...], w_out[...], preferred_element_type=_F32)
    x2_ref[0] = x1_ref[0] + _rms(mixed, post_g[...])


def _mixer_tail(x1, branch_outs, glu, conv_w, conv_b, ln_g, ln_b, w_out, post_g):
    batch, seq, _ = x1.shape
    rows = TAIL_TILE
    halo_per_tile = rows // CONV_HALO
    last_halo = seq // CONV_HALO - 1
    tile = lambda w, d=1: pl.BlockSpec((1, rows // d, d * w), lambda b, i: (b, i, 0))
    prev = pl.BlockSpec((1, CONV_HALO, CONV_WIDTH), lambda b, i: (b, jnp.maximum(i * halo_per_tile - 1, 0), 0))
    nxt = pl.BlockSpec((1, CONV_HALO, CONV_WIDTH),
                       lambda b, i: (b, jnp.minimum((i + 1) * halo_per_tile, last_halo), 0))
    outs = [o for o, _ in branch_outs]
    lses = [l for _, l in branch_outs]
    views = lambda w: [tile(w, d) for d in DILATIONS]
    weights = (conv_w, conv_b, ln_g, ln_b, w_out, post_g)
    return pl.pallas_call(
        functools.partial(_mixer_tail_kernel, rows=rows),
        name="mixer_tail",
        grid=(batch, seq // rows),
        in_specs=[tile(D_MODEL)] + views(ATTN_WIDTH) + views(LANES) + [prev, tile(CONV_WIDTH), nxt]
                 + [_whole_vmem() for _ in weights],
        out_specs=tile(D_MODEL),
        out_shape=jax.ShapeDtypeStruct((batch, seq, D_MODEL), _F32),
        scratch_shapes=[pltpu.VMEM((2, rows, LANES), _F32),
                        pltpu.VMEM((2, rows, LANES), _F32),
                        pltpu.VMEM((2 * LANES, ATTN_WIDTH), _BF16),
                        pltpu.VMEM((rows + CONV_SPAN, CONV_WIDTH), _F32),
                        pltpu.VMEM((SUBLANES, CONV_TILE + CONV_SPAN - SUBLANES, CONV_WIDTH), _F32),
                        pltpu.VMEM((rows, ATTN_WIDTH + CONV_WIDTH), _BF16)],
        compiler_params=pltpu.CompilerParams(dimension_semantics=("arbitrary",) * 2,
                                             vmem_limit_bytes=VMEM_LIMIT_BYTES),
    )(x1, *outs, *lses, glu, glu, glu, *weights)


def _ffn_weights(pre_g, w_gu, w_down, post_g):
    post_g = post_g * FFN_RESIDUAL_WEIGHT
    return (pre_g.reshape(1, D_MODEL), w_gu.astype(_BF16), w_down.astype(_BF16), post_g.reshape(1, D_MODEL))


def _layer(x, ffn1, mix_g, w_in_parts, conv_w, conv_b, ln_g, ln_b, w_out, mix_post_g, ffn2, final_g):
    batch, seq, _ = x.shape
    tokens = batch * seq
    qkv_views = [(d, ATTN_WIDTH, _BF16) for _ in range(3) for d in DILATIONS]
    x1, *qkv, glu = _ffn_call(
        _ffn_proj_kernel, "ffn_proj", FFN_TILE, x.reshape(tokens, D_MODEL), ffn1 + (mix_g,) + w_in_parts,
        [(1, D_MODEL, _F32)] + qkv_views + [(1, CONV_WIDTH, _F32)],
        extra_scratch=[pltpu.VMEM((LANE_CHUNKS, FFN_TILE, LANES), _F32)])
    per_batch = lambda a: a.reshape(batch, a.shape[0] // batch, a.shape[1])
    n = len(DILATIONS)
    branch_outs = [_attn_branch(per_batch(qkv[b]), per_batch(qkv[n + b]), per_batch(qkv[2 * n + b]), d)
                   for b, d in enumerate(DILATIONS)]
    x2 = _mixer_tail(per_batch(x1), branch_outs, per_batch(glu), conv_w, conv_b, ln_g, ln_b, w_out, mix_post_g)
    (y,) = _ffn_call(_ffn_final_kernel, "ffn_final", FFN_FINAL_TILE, x2.reshape(tokens, D_MODEL),
                     ffn2 + (final_g,), [(1, D_MODEL, _F32)])
    return y.reshape(batch, seq, D_MODEL)


def _layer_weights(l, ffn1_pre_g, ffn1_w_gu, ffn1_w_down, ffn1_post_g, mix_pre_g, w_in, conv_w, conv_b,
                   conv_ln_g, conv_ln_b, w_out, mix_post_g, ffn2_pre_g, ffn2_w_gu, ffn2_w_down, ffn2_post_g,
                   final_g):
    w_in_l = w_in[l].astype(_BF16)
    splits = (0, ATTN_WIDTH, 2 * ATTN_WIDTH, 3 * ATTN_WIDTH, 3 * ATTN_WIDTH + CONV_WIDTH,
              3 * ATTN_WIDTH + 2 * CONV_WIDTH)
    w_in_parts = tuple(w_in_l[:, a:b] for a, b in zip(splits[:-1], splits[1:]))
    row = lambda g, n: g[l].reshape(1, n)
    return (_ffn_weights(ffn1_pre_g[l], ffn1_w_gu[l], ffn1_w_down[l], ffn1_post_g[l]),
            row(mix_pre_g, D_MODEL), w_in_parts,
            conv_w[l].reshape(CONV_KERNEL, CONV_WIDTH), row(conv_b, CONV_WIDTH),
            row(conv_ln_g, CONV_WIDTH), row(conv_ln_b, CONV_WIDTH),
            w_out[l].astype(_BF16), row(mix_post_g, D_MODEL),
            _ffn_weights(ffn2_pre_g[l], ffn2_w_gu[l], ffn2_w_down[l], ffn2_post_g[l]),
            row(final_g, D_MODEL))


def kernel(x_prompt, x_sample, ffn1_pre_g, ffn1_w_gu, ffn1_w_down, ffn1_post_g, mix_pre_g, w_in, conv_w, conv_b, conv_ln_g, conv_ln_b, w_out, mix_post_g, ffn2_pre_g, ffn2_w_gu, ffn2_w_down, ffn2_post_g, final_g):
    params = (ffn1_pre_g, ffn1_w_gu, ffn1_w_down, ffn1_post_g, mix_pre_g, w_in, conv_w, conv_b, conv_ln_g,
              conv_ln_b, w_out, mix_post_g, ffn2_pre_g, ffn2_w_gu, ffn2_w_down, ffn2_post_g, final_g)
    layers = [_layer_weights(l, *params) for l in range(ffn1_pre_g.shape[0])]
    outs = []
    for x in (x_prompt, x_sample):
        for layer in layers:
            x = _layer(x, *layer)
        outs.append(x)
    return tuple(outs)
```
